```python
import math
import jax, jax.numpy as jnp
from jax import lax
import numpy as np

D_MODEL = 2048
BATCH = 1
SEQ = 16384
DEPTH = 1

N_META = 16
BLOCK_Q = 128
ATT_HEAD_DIM = 64
ATT_V_DIM = 2 * ATT_HEAD_DIM
ATT_WIDTH = D_MODEL // 2
N_ATT_HEADS = ATT_WIDTH // ATT_V_DIM
CONV_WIDTH = D_MODEL // 2
CONV_K = 3
N_BRANCH = 2
QK_WIDTH = N_ATT_HEADS * 2 * ATT_HEAD_DIM
IN_WIDTH = 3 * QK_WIDTH + 3 * CONV_WIDTH
D_FF = ((8 * D_MODEL // 3 + 255) // 256) * 256
RMS_EPS = 1e-6
NEG_INF = -1e30

kernel_name = "hybrid_diffattn_shortconv_macaron"


def rms_norm(x, g):
    xf = x.astype(jnp.float32)
    y = xf * lax.rsqrt(jnp.mean(xf * xf, axis=-1, keepdims=True) + RMS_EPS)
    return (y * g.astype(jnp.float32)).astype(x.dtype)


def swiglu_ffn(h, w_gu, w_down):
    gate, up = jnp.split(h @ w_gu, 2, axis=-1)
    return (jax.nn.silu(gate) * up) @ w_down


def lambda_init_for_layer(layer_idx):
    return 0.8 - 0.6 * math.exp(-0.3 * (layer_idx - 1))


def short_conv_mixer(b_gate, c_gate, u, conv_w):
    v = c_gate * u
    L = v.shape[1]
    vp = jnp.pad(v, ((0, 0), (CONV_K - 1, 0), (0, 0)))
    y = conv_w[0] * vp[:, 0:L]
    for tap in range(1, CONV_K):
        y = y + conv_w[tap] * vp[:, tap:tap + L]
    return b_gate * y


def diff_attention(q, k, v, lam, subln_g, lambda_init):
    Bsz, L = q.shape[0], q.shape[1]
    pf = (BLOCK_Q - N_META % BLOCK_Q) % BLOCK_Q
    pb = (-(L + pf)) % BLOCK_Q
    Lp = L + pf + pb
    nb = Lp // BLOCK_Q
    qp = jnp.pad(q, ((0, 0), (pf, pb), (0, 0), (0, 0), (0, 0)))
    kp = jnp.pad(k, ((0, 0), (pf, pb), (0, 0), (0, 0), (0, 0)))
    vp = jnp.pad(v, ((0, 0), (pf, pb), (0, 0), (0, 0)))
    key_pos = jnp.arange(Lp)
    key_valid = (key_pos >= pf) & (key_pos < pf + L)
    scale = ATT_HEAD_DIM ** -0.5
    qb = qp.reshape(Bsz, nb, BLOCK_Q, N_ATT_HEADS, 2, ATT_HEAD_DIM).transpose(1, 0, 2, 3, 4, 5)

    def one_block(args):
        blk, q_blk = args
        q_pos = blk * BLOCK_Q + jnp.arange(BLOCK_Q)
        s = jnp.einsum('bqhcd,bkhcd->bchqk', q_blk, kp,
                       preferred_element_type=jnp.float32) * scale
        mask = (key_pos[None, :] <= q_pos[:, None]) & key_valid[None, :]
        s = jnp.where(mask, s, NEG_INF)
        p = jax.nn.softmax(s, axis=-1)
        a = p[:, 0] - lam * p[:, 1]
        return jnp.einsum('bhqk,bkhv->bqhv', a.astype(vp.dtype), vp)

    out = lax.map(one_block, (jnp.arange(nb), qb))
    out = out.transpose(1, 0, 2, 3, 4).reshape(Bsz, Lp, N_ATT_HEADS, ATT_V_DIM)[:, pf:pf + L]
    out = rms_norm(out, subln_g) * (1.0 - lambda_init)
    return out.reshape(Bsz, L, ATT_WIDTH)


def setup_inputs(seed: int = 0) -> dict:
    key = jax.random.key(seed)
    ks = jax.random.split(key, 24)
    D, F = D_MODEL, D_FF

    def nrm(k, shape, scale):
        return jax.random.normal(k, shape, jnp.float32) * scale

    def gain(k, shape):
        return 1.0 + 0.02 * jax.random.normal(k, shape, jnp.float32)

    return {
        "x": nrm(ks[0], (BATCH, SEQ, D), 1.0),
        "meta_tokens": nrm(ks[1], (N_META, D), 1.0),
        "norm_ffn1": gain(ks[2], (DEPTH, D)),
        "ffn1_w_gu": nrm(ks[3], (DEPTH, D, 2 * F), D ** -0.5),
        "ffn1_w_down": nrm(ks[4], (DEPTH, F, D), F ** -0.5),
        "norm_mix": gain(ks[5], (DEPTH, D)),
        "w_in": nrm(ks[6], (DEPTH, D, IN_WIDTH), D ** -0.5),
        "conv_w": nrm(ks[7], (DEPTH, CONV_K, CONV_WIDTH), CONV_K ** -0.5),
        "lambda_q1": nrm(ks[8], (DEPTH, ATT_HEAD_DIM), 0.1),
        "lambda_k1": nrm(ks[9], (DEPTH, ATT_HEAD_DIM), 0.1),
        "lambda_q2": nrm(ks[10], (DEPTH, ATT_HEAD_DIM), 0.1),
        "lambda_k2": nrm(ks[11], (DEPTH, ATT_HEAD_DIM), 0.1),
        "subln": gain(ks[12], (DEPTH, ATT_V_DIM)),
        "w_branch": nrm(ks[13], (DEPTH, N_BRANCH, ATT_WIDTH, D), ATT_WIDTH ** -0.5),
        "w_gate": nrm(ks[14], (DEPTH, D, N_BRANCH * D), D ** -0.5),
        "w_out": nrm(ks[15], (DEPTH, D, D), D ** -0.5),
        "norm_ffn2": gain(ks[16], (DEPTH, D)),
        "ffn2_w_gu": nrm(ks[17], (DEPTH, D, 2 * F), D ** -0.5),
        "ffn2_w_down": nrm(ks[18], (DEPTH, F, D), F ** -0.5),
        "norm_final": gain(ks[19], (D,)),
    }


def reference(x, meta_tokens, norm_ffn1, ffn1_w_gu, ffn1_w_down, norm_mix, w_in, conv_w,
              lambda_q1, lambda_k1, lambda_q2, lambda_k2, subln, w_branch, w_gate, w_out,
              norm_ffn2, ffn2_w_gu, ffn2_w_down, norm_final):
    Bsz = x.shape[0]
    meta = jnp.broadcast_to(meta_tokens.astype(x.dtype)[None], (Bsz, N_META, D_MODEL))
    s = jnp.concatenate([meta, x], axis=1)
    L = s.shape[1]
    splits = [QK_WIDTH, 2 * QK_WIDTH, 3 * QK_WIDTH,
              3 * QK_WIDTH + CONV_WIDTH, 3 * QK_WIDTH + 2 * CONV_WIDTH]

    for layer in range(DEPTH):
        lam_init = lambda_init_for_layer(layer + 1)
        h = rms_norm(s, norm_ffn1[layer])
        s = s + 0.5 * swiglu_ffn(h, ffn1_w_gu[layer], ffn1_w_down[layer])

        h = rms_norm(s, norm_mix[layer])
        q, k, v, b_gate, c_gate, u = jnp.split(h @ w_in[layer], splits, axis=-1)
        q = q.reshape(Bsz, L, N_ATT_HEADS, 2, ATT_HEAD_DIM)
        k = k.reshape(Bsz, L, N_ATT_HEADS, 2, ATT_HEAD_DIM)
        v = v.reshape(Bsz, L, N_ATT_HEADS, ATT_V_DIM)
        lam = (jnp.exp(jnp.sum(lambda_q1[layer].astype(jnp.float32) * lambda_k1[layer].astype(jnp.float32)))
               - jnp.exp(jnp.sum(lambda_q2[layer].astype(jnp.float32) * lambda_k2[layer].astype(jnp.float32)))
               + lam_init)
        y_att = diff_attention(q, k, v, lam, subln[layer], lam_init)
        y_conv = short_conv_mixer(b_gate, c_gate, u, conv_w[layer])

        ys = jnp.stack([y_att, y_conv], axis=2)
        z = jnp.einsum('blnw,nwd->blnd', ys, w_branch[layer])
        gates = jax.nn.sigmoid(h @ w_gate[layer]).reshape(Bsz, L, N_BRANCH, D_MODEL)
        merged = jnp.sum(gates * z, axis=2)
        s = s + merged @ w_out[layer]

        h = rms_norm(s, norm_ffn2[layer])
        s = s + 0.5 * swiglu_ffn(h, ffn2_w_gu[layer], ffn2_w_down[layer])

    s = rms_norm(s, norm_final)
    return s[:, N_META:]
```

```python
import functools
import math

import jax
import jax.numpy as jnp
from jax import lax
from jax.experimental import pallas as pl
from jax.experimental.pallas import tpu as pltpu

D_MODEL = 2048
N_META = 16
HEAD_DIM = 64
V_DIM = 2 * HEAD_DIM
ATT_WIDTH = D_MODEL // 2
N_HEADS = ATT_WIDTH // V_DIM
CONV_WIDTH = D_MODEL // 2
CONV_K = 3
QKV_WIDTH = 3 * ATT_WIDTH
D_FF = ((8 * D_MODEL // 3 + 255) // 256) * 256
RMS_EPS = 1e-6
LAMBDA_INIT = 0.8 - 0.6 * math.exp(-0.3 * 0)

SUBLANES = 8
VMEM_LIMIT_BYTES = 56 * 1024 * 1024

BF16 = jnp.bfloat16
F32 = jnp.float32


def _rms_scale(x):
    return x * lax.rsqrt(jnp.mean(x * x, axis=-1, keepdims=True) + RMS_EPS)


def _params(*semantics):
    return pltpu.CompilerParams(dimension_semantics=semantics, vmem_limit_bytes=VMEM_LIMIT_BYTES)


def _ffn_kernel(x_ref, g_ref, wg_ref, wu_ref, wd_ref, gn_ref, *refs, final):
    if final:
        o_ref, h_scr, acc_scr = refs
    else:
        s_ref, hn_ref, h_scr, acc_scr = refs
    j = pl.program_id(1)

    @pl.when(j == 0)
    def _():
        h_scr[...] = (_rms_scale(x_ref[...]) * g_ref[...]).astype(BF16)
        acc_scr[...] = jnp.zeros_like(acc_scr)

    h = h_scr[...]
    gate = jnp.dot(h, wg_ref[...], preferred_element_type=F32)
    up = jnp.dot(h, wu_ref[...], preferred_element_type=F32)
    act = (gate * jax.nn.sigmoid(gate) * up).astype(BF16)
    acc_scr[...] += jnp.dot(act, wd_ref[...], preferred_element_type=F32)

    @pl.when(j == pl.num_programs(1) - 1)
    def _():
        s = x_ref[...] + 0.5 * acc_scr[...]
        normed = _rms_scale(s) * gn_ref[...]
        if final:
            o_ref[...] = normed
        else:
            s_ref[...] = s
            hn_ref[...] = normed.astype(BF16)


def _ffn(x, g, w_gu, w_down, g_next, *, final, tm, tf=512):
    m = x.shape[0]
    n_f = D_FF // tf
    row = pl.BlockSpec((tm, D_MODEL), lambda i, j: (i, 0))
    vec = pl.BlockSpec((1, D_MODEL), lambda i, j: (0, 0))
    if final:
        out_shape = jax.ShapeDtypeStruct((m, D_MODEL), F32)
        out_specs = row
    else:
        out_shape = (jax.ShapeDtypeStruct((m, D_MODEL), F32), jax.ShapeDtypeStruct((m, D_MODEL), BF16))
        out_specs = (row, row)
    return pl.pallas_call(
        functools.partial(_ffn_kernel, final=final),
        grid=(m // tm, n_f),
        in_specs=[
            row, vec,
            pl.BlockSpec((D_MODEL, tf), lambda i, j: (0, j)),
            pl.BlockSpec((D_MODEL, tf), lambda i, j: (0, n_f + j)),
            pl.BlockSpec((tf, D_MODEL), lambda i, j: (j, 0)),
            vec,
        ],
        out_specs=out_specs,
        out_shape=out_shape,
        scratch_shapes=[pltpu.VMEM((tm, D_MODEL), BF16), pltpu.VMEM((tm, D_MODEL), F32)],
        compiler_params=_params("parallel", "arbitrary"),
        name="ffn_final" if final else "ffn_mix",
    )(x, g.reshape(1, D_MODEL), w_gu, w_gu, w_down, g_next.reshape(1, D_MODEL))


def _proj_kernel(h_ref, w_ref, o_ref):
    o_ref[...] = jnp.dot(h_ref[...], w_ref[...], preferred_element_type=F32).astype(BF16)


def _proj(h, w, *, tm, tn=512):
    m, n = h.shape[0], w.shape[1]
    return pl.pallas_call(
        _proj_kernel,
        grid=(m // tm, n // tn),
        in_specs=[pl.BlockSpec((tm, D_MODEL), lambda i, j: (i, 0)),
                  pl.BlockSpec((D_MODEL, tn), lambda i, j: (0, j))],
        out_specs=pl.BlockSpec((tm, tn), lambda i, j: (i, j)),
        out_shape=jax.ShapeDtypeStruct((m, n), BF16),
        compiler_params=_params("parallel", "arbitrary"),
        name="qkv_proj",
    )(h, w)


def _conv_kernel(h_ref, wb_ref, wc_ref, wu_ref, cw_ref, cin_ref, y_ref, cout_ref, carry_scr):
    i, j = pl.program_id(0), pl.program_id(1)
    tm = h_ref.shape[0]

    @pl.when(i == 0)
    def _():
        carry_scr[j] = cin_ref[...]

    h = h_ref[...]
    b_gate = jnp.dot(h, wb_ref[...], preferred_element_type=F32)
    v = jnp.dot(h, wc_ref[...], preferred_element_type=F32) * jnp.dot(h, wu_ref[...], preferred_element_type=F32)
    w0, w1, w2 = cw_ref[0:1, :], cw_ref[1:2, :], cw_ref[2:3, :]

    y = w0 * pltpu.roll(v, 2, axis=0) + w1 * pltpu.roll(v, 1, axis=0) + w2 * v
    y_ref[...] = (b_gate * y).astype(BF16)

    prev = carry_scr[j]
    head = v[0:SUBLANES, :]
    rows = lax.broadcasted_iota(jnp.int32, head.shape, 0)
    v1 = jnp.where(rows >= 1, pltpu.roll(head, 1, axis=0), pltpu.roll(prev, 1, axis=0))
    v2 = jnp.where(rows >= 2, pltpu.roll(head, 2, axis=0), pltpu.roll(prev, 2, axis=0))
    y_head = w0 * v2 + w1 * v1 + w2 * head
    y_ref[0:SUBLANES, :] = (b_gate[0:SUBLANES, :] * y_head).astype(BF16)

    tail = v[tm - SUBLANES:tm, :]
    carry_scr[j] = tail
    cout_ref[...] = tail


def _conv(h, w_b, w_c, w_u, conv_w, carry_in, *, tm, tn=512):
    m = h.shape[0]
    n_c = CONV_WIDTH // tn
    col = pl.BlockSpec((D_MODEL, tn), lambda i, j: (0, j))
    return pl.pallas_call(
        _conv_kernel,
        grid=(m // tm, n_c),
        in_specs=[pl.BlockSpec((tm, D_MODEL), lambda i, j: (i, 0)), col, col, col,
                  pl.BlockSpec((CONV_K, tn), lambda i, j: (0, j)),
                  pl.BlockSpec((SUBLANES, tn), lambda i, j: (0, j))],
        out_specs=(pl.BlockSpec((tm, tn), lambda i, j: (i, j)),
                   pl.BlockSpec((SUBLANES, tn), lambda i, j: (i, j))),
        out_shape=(jax.ShapeDtypeStruct((m, CONV_WIDTH), BF16),
                   jax.ShapeDtypeStruct((m // tm * SUBLANES, CONV_WIDTH), F32)),
        scratch_shapes=[pltpu.VMEM((n_c, SUBLANES, tn), F32)],
        compiler_params=_params("arbitrary", "arbitrary"),
        name="short_conv",
    )(h, w_b, w_c, w_u, conv_w, carry_in)


def _attn_kernel(q_ref, k_ref, v_ref, km_ref, vm_ref, lq1_ref, lk1_ref, lq2_ref, lk2_ref, g_ref,
                 o_ref, m_scr, l_scr, acc_scr, *, t):
    i = pl.program_id(1)
    q = q_ref[...] * (HEAD_DIM ** -0.5)
    lane = lax.broadcasted_iota(jnp.int32, q.shape, 1)
    zero = jnp.zeros_like(q)
    q2t = jnp.concatenate([jnp.where(lane < HEAD_DIM, q, zero), jnp.where(lane >= HEAD_DIM, q, zero)], axis=0)

    def scores(k):
        return lax.dot_general(q2t, k, (((1,), (1,)), ((), ())), preferred_element_type=F32)

    s = scores(km_ref[...])
    m0 = jnp.max(s, axis=-1, keepdims=True)
    p = jnp.exp(s - m0)
    m_scr[...] = m0
    l_scr[...] = jnp.sum(p, axis=-1, keepdims=True)
    acc_scr[...] = jnp.dot(p.astype(BF16), vm_ref[...], preferred_element_type=F32)

    def update(s, v):
        m_old = m_scr[...]
        m_new = jnp.maximum(m_old, jnp.max(s, axis=-1, keepdims=True))
        alpha = jnp.exp(m_old - m_new)
        p = jnp.exp(s - m_new)
        m_scr[...] = m_new
        l_scr[...] = alpha * l_scr[...] + jnp.sum(p, axis=-1, keepdims=True)
        acc_scr[...] = alpha * acc_scr[...] + jnp.dot(p.astype(BF16), v, preferred_element_type=F32)

    def body(j, carry):
        start = pl.multiple_of(j * t, t)
        update(scores(k_ref[pl.ds(start, t), :]), v_ref[pl.ds(start, t), :])
        return carry

    lax.fori_loop(0, i, body, 0)

    start = pl.multiple_of(i * t, t)
    s = scores(k_ref[pl.ds(start, t), :])
    r = lax.broadcasted_iota(jnp.int32, s.shape, 0)
    c = lax.broadcasted_iota(jnp.int32, s.shape, 1)
    visible = c <= jnp.where(r >= t, r - t, r)
    update(jnp.where(visible, s, -1e30), v_ref[pl.ds(start, t), :])

    lam = (jnp.exp(jnp.sum(lq1_ref[...] * lk1_ref[...], axis=-1, keepdims=True))
           - jnp.exp(jnp.sum(lq2_ref[...] * lk2_ref[...], axis=-1, keepdims=True)) + LAMBDA_INIT)
    o = acc_scr[...] / l_scr[...]
    out = o[0:t, :] - lam * o[t:2 * t, :]
    o_ref[...] = ((_rms_scale(out) * g_ref[...]) * (1.0 - LAMBDA_INIT)).astype(BF16)


def _attention(qkv, qkv_meta, lq1, lk1, lq2, lk2, subln, *, t=512):
    m = qkv.shape[0]
    kv_blk = ATT_WIDTH // V_DIM
    lam_vec = pl.BlockSpec((1, HEAD_DIM), lambda h, i: (0, 0))
    return pl.pallas_call(
        functools.partial(_attn_kernel, t=t),
        grid=(N_HEADS, m // t),
        in_specs=[
            pl.BlockSpec((t, V_DIM), lambda h, i: (i, h)),
            pl.BlockSpec((m, V_DIM), lambda h, i: (0, kv_blk + h)),
            pl.BlockSpec((m, V_DIM), lambda h, i: (0, 2 * kv_blk + h)),
            pl.BlockSpec((N_META, V_DIM), lambda h, i: (0, kv_blk + h)),
            pl.BlockSpec((N_META, V_DIM), lambda h, i: (0, 2 * kv_blk + h)),
            lam_vec, lam_vec, lam_vec, lam_vec,
            pl.BlockSpec((1, V_DIM), lambda h, i: (0, 0)),
        ],
        out_specs=pl.BlockSpec((t, V_DIM), lambda h, i: (i, h)),
        out_shape=jax.ShapeDtypeStruct((m, ATT_WIDTH), BF16),
        scratch_shapes=[pltpu.VMEM((2 * t, 1), F32), pltpu.VMEM((2 * t, 1), F32),
                        pltpu.VMEM((2 * t, V_DIM), F32)],
        compiler_params=_params("parallel", "arbitrary"),
        name="diff_attention",
    )(qkv, qkv, qkv, qkv_meta, qkv_meta,
      lq1.reshape(1, HEAD_DIM), lk1.reshape(1, HEAD_DIM), lq2.reshape(1, HEAD_DIM), lk2.reshape(1, HEAD_DIM),
      subln.reshape(1, V_DIM))


def _merge_kernel(h_ref, ya_ref, yc_ref, s_ref, wg0_ref, wg1_ref, wb0_ref, wb1_ref, wo_ref, o_ref, acc_scr):
    j = pl.program_id(1)

    @pl.when(j == 0)
    def _():
        acc_scr[...] = jnp.zeros_like(acc_scr)

    h = h_ref[...]
    gate_a = jax.nn.sigmoid(jnp.dot(h, wg0_ref[...], preferred_element_type=F32))
    gate_c = jax.nn.sigmoid(jnp.dot(h, wg1_ref[...], preferred_element_type=F32))
    z_a = jnp.dot(ya_ref[...], wb0_ref[...], preferred_element_type=F32)
    z_c = jnp.dot(yc_ref[...], wb1_ref[...], preferred_element_type=F32)
    merged = (gate_a * z_a + gate_c * z_c).astype(BF16)
    acc_scr[...] += jnp.dot(merged, wo_ref[...], preferred_element_type=F32)

    @pl.when(j == pl.num_programs(1) - 1)
    def _():
        o_ref[...] = s_ref[...] + acc_scr[...]


def _merge(h, y_att, y_conv, s, w_gate, w_branch, w_out, *, tm=512, tn=512):
    m = h.shape[0]
    n_d = D_MODEL // tn
    return pl.pallas_call(
        _merge_kernel,
        grid=(m // tm, n_d),
        in_specs=[
            pl.BlockSpec((tm, D_MODEL), lambda i, j: (i, 0)),
            pl.BlockSpec((tm, ATT_WIDTH), lambda i, j: (i, 0)),
            pl.BlockSpec((tm, CONV_WIDTH), lambda i, j: (i, 0)),
            pl.BlockSpec((tm, D_MODEL), lambda i, j: (i, 0)),
            pl.BlockSpec((D_MODEL, tn), lambda i, j: (0, j)),
            pl.BlockSpec((D_MODEL, tn), lambda i, j: (0, n_d + j)),
            pl.BlockSpec((None, ATT_WIDTH, tn), lambda i, j: (0, 0, j)),
            pl.BlockSpec((None, CONV_WIDTH, tn), lambda i, j: (1, 0, j)),
            pl.BlockSpec((tn, D_MODEL), lambda i, j: (j, 0)),
        ],
        out_specs=pl.BlockSpec((tm, D_MODEL), lambda i, j: (i, 0)),
        out_shape=jax.ShapeDtypeStruct((m, D_MODEL), F32),
        scratch_shapes=[pltpu.VMEM((tm, D_MODEL), F32)],
        compiler_params=_params("parallel", "arbitrary"),
        name="gated_merge",
    )(h, y_att, y_conv, s, w_gate, w_gate, w_branch, w_branch, w_out)


def kernel(x, meta_tokens, norm_ffn1, ffn1_w_gu, ffn1_w_down, norm_mix, w_in, conv_w, lambda_q1, lambda_k1,
           lambda_q2, lambda_k2, subln, w_branch, w_gate, w_out, norm_ffn2, ffn2_w_gu, ffn2_w_down, norm_final):
    bsz, seq, _ = x.shape
    assert bsz == 1 and meta_tokens.shape == (N_META, D_MODEL)
    xs = x.reshape(seq, D_MODEL)
    w_in = w_in[0].astype(BF16)
    w_qkv, w_b, w_c, w_u = (w_in[:, :QKV_WIDTH], w_in[:, QKV_WIDTH:QKV_WIDTH + CONV_WIDTH],
                            w_in[:, QKV_WIDTH + CONV_WIDTH:QKV_WIDTH + 2 * CONV_WIDTH],
                            w_in[:, QKV_WIDTH + 2 * CONV_WIDTH:])
    w_gu1, w_d1 = ffn1_w_gu[0].astype(BF16), ffn1_w_down[0].astype(BF16)

    def mixer_inputs(tokens, carry_in, tm):
        s1, h = _ffn(tokens, norm_ffn1[0], w_gu1, w_d1, norm_mix[0], final=False, tm=tm)
        qkv = _proj(h, w_qkv, tm=tm)
        y_conv, carry_out = _conv(h, w_b, w_c, w_u, conv_w[0], carry_in, tm=tm)
        return s1, h, qkv, y_conv, carry_out

    zero_carry = jnp.zeros((SUBLANES, CONV_WIDTH), F32)
    _, _, qkv_meta, _, meta_carry = mixer_inputs(meta_tokens.astype(x.dtype), zero_carry, N_META)
    s1, h, qkv, y_conv, _ = mixer_inputs(xs, meta_carry, 512)

    y_att = _attention(qkv, qkv_meta, lambda_q1[0], lambda_k1[0], lambda_q2[0], lambda_k2[0], subln[0])
    s2 = _merge(h, y_att, y_conv, s1, w_gate[0].astype(BF16), w_branch[0].astype(BF16), w_out[0].astype(BF16))
    out = _ffn(s2, norm_ffn2[0], ffn2_w_gu[0].astype(BF16), ffn2_w_down[0].astype(BF16), norm_final,
               final=True, tm=512)
    return out.reshape(bsz, seq, D_MODEL)
```

```python
import functools
import math

import jax
import jax.numpy as jnp
from jax import lax
from jax.experimental import pallas as pl
from jax.experimental.pallas import tpu as pltpu

D_MODEL = 2048
N_META = 16
HEAD_DIM = 64
V_DIM = 2 * HEAD_DIM
ATT_WIDTH = D_MODEL // 2
N_HEADS = ATT_WIDTH // V_DIM
CONV_WIDTH = D_MODEL // 2
CONV_K = 3
D_FF = ((8 * D_MODEL // 3 + 255) // 256) * 256
RMS_EPS = 1e-6
LAMBDA_INIT = 0.8 - 0.6 * math.exp(-0.3 * 0)
MASKED = -1e30
Q_SCALE = math.log2(math.e) * HEAD_DIM ** -0.5

SUBLANES = 8
MXU_WIDTH = 256
VMEM_LIMIT_BYTES = 56 * 1024 * 1024

BF16 = jnp.bfloat16
F32 = jnp.float32
NT_DIMS = (((1,), (1,)), ((), ()))


def _rms_scale(x):
    return x * lax.rsqrt(jnp.mean(x * x, axis=-1, keepdims=True) + RMS_EPS)


def _params(*semantics):
    return pltpu.CompilerParams(dimension_semantics=semantics, vmem_limit_bytes=VMEM_LIMIT_BYTES)


def _ffn_kernel(x_ref, g_ref, wg_ref, wu_ref, wd_ref, gn_ref, *refs, final):
    if final:
        o_ref, h_scr, acc_scr = refs
    else:
        s_ref, hn_ref, h_scr, acc_scr = refs
    j = pl.program_id(1)

    @pl.when(j == 0)
    def _():
        h_scr[...] = (_rms_scale(x_ref[...]) * g_ref[...]).astype(BF16)
        acc_scr[...] = jnp.zeros_like(acc_scr)

    h = h_scr[...]
    gate = jnp.dot(h, wg_ref[...], preferred_element_type=F32)
    up = jnp.dot(h, wu_ref[...], preferred_element_type=F32)
    act = (gate * jax.nn.sigmoid(gate) * up).astype(BF16)
    acc_scr[...] += jnp.dot(act, wd_ref[...], preferred_element_type=F32)

    @pl.when(j == pl.num_programs(1) - 1)
    def _():
        s = x_ref[...] + 0.5 * acc_scr[...]
        normed = _rms_scale(s) * gn_ref[...]
        if final:
            o_ref[...] = normed
        else:
            s_ref[...] = s
            hn_ref[...] = normed.astype(BF16)


def _ffn(x, g, w_gu, w_down, g_next, *, final, tm, tf=512):
    m = x.shape[0]
    n_f = D_FF // tf
    row = pl.BlockSpec((tm, D_MODEL), lambda i, j: (i, 0))
    vec = pl.BlockSpec((1, D_MODEL), lambda i, j: (0, 0))
    if final:
        out_shape = jax.ShapeDtypeStruct((m, D_MODEL), F32)
        out_specs = row
    else:
        out_shape = (jax.ShapeDtypeStruct((m, D_MODEL), F32), jax.ShapeDtypeStruct((m, D_MODEL), BF16))
        out_specs = (row, row)
    return pl.pallas_call(
        functools.partial(_ffn_kernel, final=final),
        grid=(m // tm, n_f),
        in_specs=[
            row, vec,
            pl.BlockSpec((D_MODEL, tf), lambda i, j: (0, j)),
            pl.BlockSpec((D_MODEL, tf), lambda i, j: (0, n_f + j)),
            pl.BlockSpec((tf, D_MODEL), lambda i, j: (j, 0)),
            vec,
        ],
        out_specs=out_specs,
        out_shape=out_shape,
        scratch_shapes=[pltpu.VMEM((tm, D_MODEL), BF16), pltpu.VMEM((tm, D_MODEL), F32)],
        compiler_params=_params("parallel", "arbitrary"),
        name="ffn_final" if final else "ffn_mix",
    )(x, g.reshape(1, D_MODEL), w_gu, w_gu, w_down, g_next.reshape(1, D_MODEL))


def _proj_kernel(h_ref, w_ref, o_ref, *, n_q_chunks):
    scale = jnp.where(pl.program_id(1) < n_q_chunks, Q_SCALE, 1.0)
    o_ref[...] = (jnp.dot(h_ref[...], w_ref[...], preferred_element_type=F32) * scale).astype(BF16)


def _proj(h, w, *, tm, tn=512):
    m, n = h.shape[0], w.shape[1]
    return pl.pallas_call(
        functools.partial(_proj_kernel, n_q_chunks=ATT_WIDTH // tn),
        grid=(m // tm, n // tn),
        in_specs=[pl.BlockSpec((tm, D_MODEL), lambda i, j: (i, 0)),
                  pl.BlockSpec((D_MODEL, tn), lambda i, j: (0, j))],
        out_specs=pl.BlockSpec((tm, tn), lambda i, j: (i, j)),
        out_shape=jax.ShapeDtypeStruct((m, n), BF16),
        compiler_params=_params("parallel", "arbitrary"),
        name="qk_proj",
    )(h, w)


def _proj_t_kernel(wt_ref, h_ref, o_ref):
    o_ref[...] = lax.dot_general(wt_ref[...], h_ref[...], NT_DIMS, preferred_element_type=F32).astype(BF16)


def _proj_t(h, w_t, *, tm):
    m, n = h.shape[0], w_t.shape[0]
    return pl.pallas_call(
        _proj_t_kernel,
        grid=(m // tm,),
        in_specs=[pl.BlockSpec((n, D_MODEL), lambda i: (0, 0)),
                  pl.BlockSpec((tm, D_MODEL), lambda i: (i, 0))],
        out_specs=pl.BlockSpec((n, tm), lambda i: (0, i)),
        out_shape=jax.ShapeDtypeStruct((n, m), BF16),
        compiler_params=_params("parallel"),
        name="v_proj_t",
    )(w_t, h)


def _conv_kernel(h_ref, wb_ref, wc_ref, wu_ref, cw_ref, cin_ref, y_ref, cout_ref, carry_scr):
    i, j = pl.program_id(0), pl.program_id(1)
    tm = h_ref.shape[0]

    @pl.when(i == 0)
    def _():
        carry_scr[j] = cin_ref[...]

    h = h_ref[...]
    b_gate = jnp.dot(h, wb_ref[...], preferred_element_type=F32)
    v = jnp.dot(h, wc_ref[...], preferred_element_type=F32) * jnp.dot(h, wu_ref[...], preferred_element_type=F32)
    w0, w1, w2 = cw_ref[0:1, :], cw_ref[1:2, :], cw_ref[2:3, :]

    y = w0 * pltpu.roll(v, 2, axis=0) + w1 * pltpu.roll(v, 1, axis=0) + w2 * v
    y_ref[...] = (b_gate * y).astype(BF16)

    prev = carry_scr[j]
    head = v[0:SUBLANES, :]
    rows = lax.broadcasted_iota(jnp.int32, head.shape, 0)
    v1 = jnp.where(rows >= 1, pltpu.roll(head, 1, axis=0), pltpu.roll(prev, 1, axis=0))
    v2 = jnp.where(rows >= 2, pltpu.roll(head, 2, axis=0), pltpu.roll(prev, 2, axis=0))
    y_head = w0 * v2 + w1 * v1 + w2 * head
    y_ref[0:SUBLANES, :] = (b_gate[0:SUBLANES, :] * y_head).astype(BF16)

    tail = v[tm - SUBLANES:tm, :]
    carry_scr[j] = tail
    cout_ref[...] = tail


def _conv(h, w_b, w_c, w_u, conv_w, carry_in, *, tm, tn=512):
    m = h.shape[0]
    n_c = CONV_WIDTH // tn
    col = pl.BlockSpec((D_MODEL, tn), lambda i, j: (0, j))
    return pl.pallas_call(
        _conv_kernel,
        grid=(m // tm, n_c),
        in_specs=[pl.BlockSpec((tm, D_MODEL), lambda i, j: (i, 0)), col, col, col,
                  pl.BlockSpec((CONV_K, tn), lambda i, j: (0, j)),
                  pl.BlockSpec((SUBLANES, tn), lambda i, j: (0, j))],
        out_specs=(pl.BlockSpec((tm, tn), lambda i, j: (i, j)),
                   pl.BlockSpec((SUBLANES, tn), lambda i, j: (i, j))),
        out_shape=(jax.ShapeDtypeStruct((m, CONV_WIDTH), BF16),
                   jax.ShapeDtypeStruct((m // tm * SUBLANES, CONV_WIDTH), F32)),
        scratch_shapes=[pltpu.VMEM((n_c, SUBLANES, tn), F32)],
        compiler_params=_params("arbitrary", "arbitrary"),
        name="short_conv",
    )(h, w_b, w_c, w_u, conv_w, carry_in)


def _attn_kernel(q_ref, k_ref, vt_ref, km_ref, vtm_ref, lq1_ref, lk1_ref, lq2_ref, lk2_ref, g_ref,
                 o_ref, q2_scr, s_scr, m_scr, l_scr, acc_scr, *, t, tk, cq):
    i = pl.program_id(1)
    n_chunks = 2 * t // cq

    q = q_ref[...]
    lane = lax.broadcasted_iota(jnp.int32, q.shape, 1)
    zero = jnp.zeros_like(q)
    q2_scr[0:t, :] = jnp.where(lane < HEAD_DIM, q, zero)
    q2_scr[t:2 * t, :] = jnp.where(lane >= HEAD_DIM, q, zero)

    def scores(k, c):
        return lax.dot_general(k, q2_scr[c * cq:(c + 1) * cq, :], NT_DIMS, preferred_element_type=F32)

    def chunk_update(c, s_ref, vt, first=False, key_offset=None):
        cols = slice(c * cq, (c + 1) * cq)

        def load_scores():
            s = s_ref[...]
            if key_offset is None:
                return s
            key = key_offset + lax.broadcasted_iota(jnp.int32, s.shape, 0)
            query = (c * cq) % t + lax.broadcasted_iota(jnp.int32, s.shape, 1)
            return jnp.where(key <= query, s, MASKED)

        m_blk = jnp.max(load_scores(), axis=0, keepdims=True)
        if first:
            m_new = m_blk
        else:
            m_old = m_scr[:, cols]
            m_new = jnp.maximum(m_old, m_blk)
            alpha = jnp.exp2(m_old - m_new)
        p = jnp.exp2(load_scores() - m_new)
        l_blk = jnp.sum(p, axis=0, keepdims=True)
        pv = jnp.dot(vt, p.astype(BF16), preferred_element_type=F32)
        m_scr[:, cols] = m_new
        if first:
            l_scr[:, cols] = l_blk
            acc_scr[:, cols] = pv
        else:
            l_scr[:, cols] = alpha * l_scr[:, cols] + l_blk
            acc_scr[:, cols] = alpha * acc_scr[:, cols] + pv

    meta_s = s_scr.at[0, 0:N_META, :]
    for c in range(n_chunks):
        meta_s[...] = scores(km_ref[...], c)
        chunk_update(c, meta_s, vtm_ref[...], first=True)

    def run(base, items, next_item):
        assert len(items) % 2 == 0
        for n, (rel, c, key_offset) in enumerate(items):
            following = items[n + 1] if n + 1 < len(items) else next_item
            if following is not None:
                s_scr[(n + 1) % 2] = scores(k_ref[pl.ds(base + following[0], tk), :], following[1])
            chunk_update(c, s_scr.at[n % 2], vt_ref[:, pl.ds(base + rel, tk)], key_offset=key_offset)

    visible_items = [(d * tk, c, None) for d in range(t // tk) for c in range(n_chunks)]
    diagonal_items = []
    for d in range(t // tk):
        for c in range(n_chunks):
            first_query = (c * cq) % t
            if d * tk <= first_query + cq - 1:
                fully_visible = d * tk + tk - 1 <= first_query
                diagonal_items.append((d * tk, c, None if fully_visible else d * tk))

    s_scr[0] = scores(k_ref[0:tk, :], 0)

    def body(g, carry):
        run(pl.multiple_of(g * t, t), visible_items, (t, 0))
        return carry

    lax.fori_loop(0, i, body, 0)
    run(pl.multiple_of(i * t, t), diagonal_items, None)

    lam = (jnp.exp(jnp.sum(lq1_ref[...] * lk1_ref[...], axis=-1, keepdims=True))
           - jnp.exp(jnp.sum(lq2_ref[...] * lk2_ref[...], axis=-1, keepdims=True)) + LAMBDA_INIT)
    o = acc_scr[...] / l_scr[...]
    out = o[:, 0:t] - lam * o[:, t:2 * t]
    normed = out * lax.rsqrt(jnp.mean(out * out, axis=0, keepdims=True) + RMS_EPS)
    o_ref[...] = ((normed * g_ref[...]) * (1.0 - LAMBDA_INIT)).T.astype(BF16)


def _attention(qk, vt, qk_meta, vt_meta, lq1, lk1, lq2, lk2, subln, *, t=1024, tk=512, cq=2 * MXU_WIDTH):
    m = qk.shape[0]
    lam_vec = pl.BlockSpec((1, HEAD_DIM), lambda h, i: (0, 0))
    return pl.pallas_call(
        functools.partial(_attn_kernel, t=t, tk=tk, cq=cq),
        grid=(N_HEADS, m // t),
        in_specs=[
            pl.BlockSpec((t, V_DIM), lambda h, i: (i, h)),
            pl.BlockSpec((m, V_DIM), lambda h, i: (0, N_HEADS + h)),
            pl.BlockSpec((V_DIM, m), lambda h, i: (h, 0)),
            pl.BlockSpec((N_META, V_DIM), lambda h, i: (0, N_HEADS + h)),
            pl.BlockSpec((V_DIM, N_META), lambda h, i: (h, 0)),
            lam_vec, lam_vec, lam_vec, lam_vec,
            pl.BlockSpec((V_DIM, 1), lambda h, i: (0, 0)),
        ],
        out_specs=pl.BlockSpec((t, V_DIM), lambda h, i: (i, h)),
        out_shape=jax.ShapeDtypeStruct((m, ATT_WIDTH), BF16),
        scratch_shapes=[pltpu.VMEM((2 * t, V_DIM), BF16), pltpu.VMEM((2, tk, cq), F32),
                        pltpu.VMEM((1, 2 * t), F32),
                        pltpu.VMEM((1, 2 * t), F32), pltpu.VMEM((V_DIM, 2 * t), F32)],
        compiler_params=_params("parallel", "arbitrary"),
        name="diff_attention",
    )(qk, qk, vt, qk_meta, vt_meta,
      lq1.reshape(1, HEAD_DIM), lk1.reshape(1, HEAD_DIM), lq2.reshape(1, HEAD_DIM), lk2.reshape(1, HEAD_DIM),
      subln.reshape(V_DIM, 1))


def _merge_kernel(h_ref, ya_ref, yc_ref, s_ref, wg0_ref, wg1_ref, wb0_ref, wb1_ref, wo_ref, o_ref, acc_scr):
    j = pl.program_id(1)

    @pl.when(j == 0)
    def _():
        acc_scr[...] = jnp.zeros_like(acc_scr)

    h = h_ref[...]
    gate_a = jax.nn.sigmoid(jnp.dot(h, wg0_ref[...], preferred_element_type=F32))
    gate_c = jax.nn.sigmoid(jnp.dot(h, wg1_ref[...], preferred_element_type=F32))
    z_a = jnp.dot(ya_ref[...], wb0_ref[...], preferred_element_type=F32)
    z_c = jnp.dot(yc_ref[...], wb1_ref[...], preferred_element_type=F32)
    merged = (gate_a * z_a + gate_c * z_c).astype(BF16)
    acc_scr[...] += jnp.dot(merged, wo_ref[...], preferred_element_type=F32)

    @pl.when(j == pl.num_programs(1) - 1)
    def _():
        o_ref[...] = s_ref[...] + acc_scr[...]


def _merge(h, y_att, y_conv, s, w_gate, w_branch, w_out, *, tm=512, tn=512):
    m = h.shape[0]
    n_d = D_MODEL // tn
    return pl.pallas_call(
        _merge_kernel,
        grid=(m // tm, n_d),
        in_specs=[
            pl.BlockSpec((tm, D_MODEL), lambda i, j: (i, 0)),
            pl.BlockSpec((tm, ATT_WIDTH), lambda i, j: (i, 0)),
            pl.BlockSpec((tm, CONV_WIDTH), lambda i, j: (i, 0)),
            pl.BlockSpec((tm, D_MODEL), lambda i, j: (i, 0)),
            pl.BlockSpec((D_MODEL, tn), lambda i, j: (0, j)),
            pl.BlockSpec((D_MODEL, tn), lambda i, j: (0, n_d + j)),
            pl.BlockSpec((None, ATT_WIDTH, tn), lambda i, j: (0, 0, j)),
            pl.BlockSpec((None, CONV_WIDTH, tn), lambda i, j: (1, 0, j)),
            pl.BlockSpec((tn, D_MODEL), lambda i, j: (j, 0)),
        ],
        out_specs=pl.BlockSpec((tm, D_MODEL), lambda i, j: (i, 0)),
        out_shape=jax.ShapeDtypeStruct((m, D_MODEL), F32),
        scratch_shapes=[pltpu.VMEM((tm, D_MODEL), F32)],
        compiler_params=_params("parallel", "arbitrary"),
        name="gated_merge",
    )(h, y_att, y_conv, s, w_gate, w_gate, w_branch, w_branch, w_out)


def kernel(x, meta_tokens, norm_ffn1, ffn1_w_gu, ffn1_w_down, norm_mix, w_in, conv_w, lambda_q1, lambda_k1,
           lambda_q2, lambda_k2, subln, w_branch, w_gate, w_out, norm_ffn2, ffn2_w_gu, ffn2_w_down, norm_final):
    bsz, seq, _ = x.shape
    assert bsz == 1 and meta_tokens.shape == (N_META, D_MODEL)
    xs = x.reshape(seq, D_MODEL)
    w_in = w_in[0].astype(BF16)
    c0 = 3 * ATT_WIDTH
    w_qk, w_vt = w_in[:, :2 * ATT_WIDTH], w_in[:, 2 * ATT_WIDTH:c0].T
    w_b, w_c, w_u = (w_in[:, c0:c0 + CONV_WIDTH], w_in[:, c0 + CONV_WIDTH:c0 + 2 * CONV_WIDTH],
                     w_in[:, c0 + 2 * CONV_WIDTH:])
    w_gu1, w_d1 = ffn1_w_gu[0].astype(BF16), ffn1_w_down[0].astype(BF16)

    def mixer_inputs(tokens, carry_in, tm):
        s1, h = _ffn(tokens, norm_ffn1[0], w_gu1, w_d1, norm_mix[0], final=False, tm=tm)
        qk = _proj(h, w_qk, tm=tm)
        vt = _proj_t(h, w_vt, tm=tm)
        y_conv, tails = _conv(h, w_b, w_c, w_u, conv_w[0], carry_in, tm=tm)
        return s1, h, qk, vt, y_conv, tails

    zero_carry = jnp.zeros((SUBLANES, CONV_WIDTH), F32)
    _, _, qk_meta, vt_meta, _, meta_tail = mixer_inputs(meta_tokens.astype(x.dtype), zero_carry, N_META)
    s1, h, qk, vt, y_conv, _ = mixer_inputs(xs, meta_tail, 512)

    y_att = _attention(qk, vt, qk_meta, vt_meta, lambda_q1[0], lambda_k1[0], lambda_q2[0], lambda_k2[0], subln[0])
    s2 = _merge(h, y_att, y_conv, s1, w_gate[0].astype(BF16), w_branch[0].astype(BF16), w_out[0].astype(BF16))
    out = _ffn(s2, norm_ffn2[0], ffn2_w_gu[0].astype(BF16), ffn2_w_down[0].astype(BF16), norm_final,
               final=True, tm=512)
    return out.reshape(bsz, seq, D_MODEL)
```

```python
import functools
import math

import jax
import jax.numpy as jnp
from jax import lax
from jax.experimental import pallas as pl
from jax.experimental.pallas import tpu as pltpu

D_MODEL = 2048
N_META = 16
HEAD_DIM = 64
V_DIM = 2 * HEAD_DIM
ATT_WIDTH = D_MODEL // 2
N_HEADS = ATT_WIDTH // V_DIM
CONV_WIDTH = D_MODEL // 2
CONV_K = 3
D_FF = ((8 * D_MODEL // 3 + 255) // 256) * 256
RMS_EPS = 1e-6
LAMBDA_INIT = 0.8 - 0.6 * math.exp(-0.3 * 0)
MASKED = -1e30
Q_SCALE = math.log2(math.e) * HEAD_DIM ** -0.5

SUBLANES = 8
MXU_WIDTH = 256
VMEM_LIMIT_BYTES = 56 * 1024 * 1024

BF16 = jnp.bfloat16
F32 = jnp.float32
NT_DIMS = (((1,), (1,)), ((), ()))


def _rms_scale(x):
    return x * lax.rsqrt(jnp.mean(x * x, axis=-1, keepdims=True) + RMS_EPS)


def _params(*semantics):
    return pltpu.CompilerParams(dimension_semantics=semantics, vmem_limit_bytes=VMEM_LIMIT_BYTES)


def _ffn_kernel(x_ref, g_ref, wg_ref, wu_ref, wd_ref, gn_ref, *refs, final):
    if final:
        o_ref, h_scr, acc_scr = refs
    else:
        s_ref, hn_ref, h_scr, acc_scr = refs
    j = pl.program_id(1)

    @pl.when(j == 0)
    def _():
        h_scr[...] = (_rms_scale(x_ref[...]) * g_ref[...]).astype(BF16)
        acc_scr[...] = jnp.zeros_like(acc_scr)

    h = h_scr[...]
    gate = jnp.dot(h, wg_ref[...], preferred_element_type=F32)
    up = jnp.dot(h, wu_ref[...], preferred_element_type=F32)
    act = (gate * jax.nn.sigmoid(gate) * up).astype(BF16)
    acc_scr[...] += jnp.dot(act, wd_ref[...], preferred_element_type=F32)

    @pl.when(j == pl.num_programs(1) - 1)
    def _():
        s = x_ref[...] + 0.5 * acc_scr[...]
        normed = _rms_scale(s) * gn_ref[...]
        if final:
            o_ref[...] = normed
        else:
            s_ref[...] = s
            hn_ref[...] = normed.astype(BF16)


def _ffn(x, g, w_gu, w_down, g_next, *, final, tm, tf=512):
    m = x.shape[0]
    n_f = D_FF // tf
    row = pl.BlockSpec((tm, D_MODEL), lambda i, j: (i, 0))
    vec = pl.BlockSpec((1, D_MODEL), lambda i, j: (0, 0))
    if final:
        out_shape = jax.ShapeDtypeStruct((m, D_MODEL), F32)
        out_specs = row
    else:
        out_shape = (jax.ShapeDtypeStruct((m, D_MODEL), F32), jax.ShapeDtypeStruct((m, D_MODEL), BF16))
        out_specs = (row, row)
    return pl.pallas_call(
        functools.partial(_ffn_kernel, final=final),
        grid=(m // tm, n_f),
        in_specs=[
            row, vec,
            pl.BlockSpec((D_MODEL, tf), lambda i, j: (0, j)),
            pl.BlockSpec((D_MODEL, tf), lambda i, j: (0, n_f + j)),
            pl.BlockSpec((tf, D_MODEL), lambda i, j: (j, 0)),
            vec,
        ],
        out_specs=out_specs,
        out_shape=out_shape,
        scratch_shapes=[pltpu.VMEM((tm, D_MODEL), BF16), pltpu.VMEM((tm, D_MODEL), F32)],
        compiler_params=_params("parallel", "arbitrary"),
        name="ffn_final" if final else "ffn_mix",
    )(x, g.reshape(1, D_MODEL), w_gu, w_gu, w_down, g_next.reshape(1, D_MODEL))


def _proj_kernel(h_ref, w_ref, o_ref, *, n_q_chunks):
    scale = jnp.where(pl.program_id(1) < n_q_chunks, Q_SCALE, 1.0)
    o_ref[...] = (jnp.dot(h_ref[...], w_ref[...], preferred_element_type=F32) * scale).astype(BF16)


def _proj(h, w, *, tm, tn=1024):
    m, n = h.shape[0], w.shape[1]
    return pl.pallas_call(
        functools.partial(_proj_kernel, n_q_chunks=ATT_WIDTH // tn),
        grid=(m // tm, n // tn),
        in_specs=[pl.BlockSpec((tm, D_MODEL), lambda i, j: (i, 0)),
                  pl.BlockSpec((D_MODEL, tn), lambda i, j: (0, j))],
        out_specs=pl.BlockSpec((tm, tn), lambda i, j: (i, j)),
        out_shape=jax.ShapeDtypeStruct((m, n), BF16),
        compiler_params=_params("parallel", "arbitrary"),
        name="qk_proj",
    )(h, w)


def _proj_t_kernel(wt_ref, h_ref, o_ref):
    o_ref[...] = lax.dot_general(wt_ref[...], h_ref[...], NT_DIMS, preferred_element_type=F32).astype(BF16)


def _proj_t(h, w_t, *, tm):
    m, n = h.shape[0], w_t.shape[0]
    return pl.pallas_call(
        _proj_t_kernel,
        grid=(m // tm,),
        in_specs=[pl.BlockSpec((n, D_MODEL), lambda i: (0, 0)),
                  pl.BlockSpec((tm, D_MODEL), lambda i: (i, 0))],
        out_specs=pl.BlockSpec((n, tm), lambda i: (0, i)),
        out_shape=jax.ShapeDtypeStruct((n, m), BF16),
        compiler_params=_params("parallel"),
        name="v_proj_t",
    )(w_t, h)


def _conv_kernel(h_ref, wb_ref, wc_ref, wu_ref, cw_ref, cin_ref, y_ref, cout_ref, carry_scr):
    i, j = pl.program_id(0), pl.program_id(1)
    tm = h_ref.shape[0]

    @pl.when(i == 0)
    def _():
        carry_scr[j] = cin_ref[...]

    h = h_ref[...]
    b_gate = jnp.dot(h, wb_ref[...], preferred_element_type=F32)
    v = jnp.dot(h, wc_ref[...], preferred_element_type=F32) * jnp.dot(h, wu_ref[...], preferred_element_type=F32)
    w0, w1, w2 = cw_ref[0:1, :], cw_ref[1:2, :], cw_ref[2:3, :]

    y = w0 * pltpu.roll(v, 2, axis=0) + w1 * pltpu.roll(v, 1, axis=0) + w2 * v
    y_ref[...] = (b_gate * y).astype(BF16)

    prev = carry_scr[j]
    head = v[0:SUBLANES, :]
    rows = lax.broadcasted_iota(jnp.int32, head.shape, 0)
    v1 = jnp.where(rows >= 1, pltpu.roll(head, 1, axis=0), pltpu.roll(prev, 1, axis=0))
    v2 = jnp.where(rows >= 2, pltpu.roll(head, 2, axis=0), pltpu.roll(prev, 2, axis=0))
    y_head = w0 * v2 + w1 * v1 + w2 * head
    y_ref[0:SUBLANES, :] = (b_gate[0:SUBLANES, :] * y_head).astype(BF16)

    tail = v[tm - SUBLANES:tm, :]
    carry_scr[j] = tail
    cout_ref[...] = tail


def _conv(h, w_b, w_c, w_u, conv_w, carry_in, *, tm, tn=1024):
    m = h.shape[0]
    n_c = CONV_WIDTH // tn
    col = pl.BlockSpec((D_MODEL, tn), lambda i, j: (0, j))
    return pl.pallas_call(
        _conv_kernel,
        grid=(m // tm, n_c),
        in_specs=[pl.BlockSpec((tm, D_MODEL), lambda i, j: (i, 0)), col, col, col,
                  pl.BlockSpec((CONV_K, tn), lambda i, j: (0, j)),
                  pl.BlockSpec((SUBLANES, tn), lambda i, j: (0, j))],
        out_specs=(pl.BlockSpec((tm, tn), lambda i, j: (i, j)),
                   pl.BlockSpec((SUBLANES, tn), lambda i, j: (i, j))),
        out_shape=(jax.ShapeDtypeStruct((m, CONV_WIDTH), BF16),
                   jax.ShapeDtypeStruct((m // tm * SUBLANES, CONV_WIDTH), F32)),
        scratch_shapes=[pltpu.VMEM((n_c, SUBLANES, tn), F32)],
        compiler_params=_params("arbitrary", "arbitrary"),
        name="short_conv",
    )(h, w_b, w_c, w_u, conv_w, carry_in)


def _attn_kernel(q_ref, k_ref, vt_ref, km_ref, vtm_ref, lq1_ref, lk1_ref, lq2_ref, lk2_ref, g_ref,
                 o_ref, q2_scr, s_scr, m_scr, l_scr, acc_scr, *, t, tk, cq):
    i = pl.program_id(1)
    n_chunks = 2 * t // cq

    q = q_ref[...]
    lane = lax.broadcasted_iota(jnp.int32, q.shape, 1)
    zero = jnp.zeros_like(q)
    q2_scr[0:t, :] = jnp.where(lane < HEAD_DIM, q, zero)
    q2_scr[t:2 * t, :] = jnp.where(lane >= HEAD_DIM, q, zero)

    def scores(k, c):
        return lax.dot_general(k, q2_scr[c * cq:(c + 1) * cq, :], NT_DIMS, preferred_element_type=F32)

    def keys_of(base, item):
        if item[0] is None:
            return km_ref[...], vtm_ref[...]
        return k_ref[pl.ds(base + item[0], tk), :], vt_ref[:, pl.ds(base + item[0], tk)]

    def chunk_update(c, s_ref, vt, key_offset):
        cols = slice(c * cq, (c + 1) * cq)

        def load_scores():
            s = s_ref[...]
            if key_offset is None:
                return s
            key = key_offset + lax.broadcasted_iota(jnp.int32, s.shape, 0)
            query = (c * cq) % t + lax.broadcasted_iota(jnp.int32, s.shape, 1)
            return jnp.where(key <= query, s, MASKED)

        m_old = m_scr[:, cols]
        m_new = jnp.maximum(m_old, jnp.max(load_scores(), axis=0, keepdims=True))
        alpha = jnp.exp2(m_old - m_new)
        p = jnp.exp2(load_scores() - m_new)
        m_scr[:, cols] = m_new
        l_scr[:, cols] = alpha * l_scr[:, cols] + jnp.sum(p, axis=0, keepdims=True)
        acc_scr[:, cols] = alpha * acc_scr[:, cols] + jnp.dot(vt, p.astype(BF16), preferred_element_type=F32)

    n_slots = s_scr.shape[0]
    ahead = n_slots // 2

    def issue_scores(base, item, slot):
        k = keys_of(base, item)[0]
        s_scr[slot, 0:k.shape[0], :] = scores(k, item[1])

    def run(base, items, next_items):
        assert len(items) % n_slots == 0 or not next_items
        upcoming = items + next_items
        for n, item in enumerate(items):
            if n + ahead < len(upcoming):
                issue_scores(base, upcoming[n + ahead], (n + ahead) % n_slots)
            vt = keys_of(base, item)[1]
            chunk_update(item[1], s_scr.at[n % n_slots, 0:vt.shape[1], :], vt, item[2])

    visible_items = [(d * tk, c, None) for d in range(t // tk) for c in range(n_chunks)]
    diagonal_items = []
    for d in range(t // tk):
        for c in range(n_chunks):
            first_query = (c * cq) % t
            if d * tk <= first_query + cq - 1:
                fully_visible = d * tk + tk - 1 <= first_query
                diagonal_items.append((d * tk, c, None if fully_visible else d * tk))
    diagonal_items += [(None, c, None) for c in range(n_chunks)]

    m_scr[...] = jnp.full_like(m_scr, MASKED)
    l_scr[...] = jnp.zeros_like(l_scr)
    acc_scr[...] = jnp.zeros_like(acc_scr)
    assert [item[:2] for item in diagonal_items[:ahead]] == [item[:2] for item in visible_items[:ahead]]
    for n in range(ahead):
        issue_scores(0, visible_items[n], n)

    def following_tile(items):
        return [(t + rel, c, key_offset) for rel, c, key_offset in items]

    def pair_body(g, carry):
        pair = visible_items + following_tile(visible_items)
        run(pl.multiple_of(2 * g * t, 2 * t), pair, following_tile(following_tile(visible_items[:ahead])))
        return carry

    lax.fori_loop(0, i // 2, pair_body, 0)

    @pl.when(i % 2 == 1)
    def _():
        run(pl.multiple_of((i - 1) * t, t), visible_items, following_tile(visible_items[:ahead]))

    run(pl.multiple_of(i * t, t), diagonal_items, [])

    lam = (jnp.exp(jnp.sum(lq1_ref[...] * lk1_ref[...], axis=-1, keepdims=True))
           - jnp.exp(jnp.sum(lq2_ref[...] * lk2_ref[...], axis=-1, keepdims=True)) + LAMBDA_INIT)
    o = acc_scr[...] / l_scr[...]
    out = o[:, 0:t] - lam * o[:, t:2 * t]
    normed = out * lax.rsqrt(jnp.mean(out * out, axis=0, keepdims=True) + RMS_EPS)
    o_ref[...] = ((normed * g_ref[...]) * (1.0 - LAMBDA_INIT)).T.astype(BF16)


def _attention(qk, vt, qk_meta, vt_meta, lq1, lk1, lq2, lk2, subln, *, t=1024, tk=512, cq=2 * MXU_WIDTH):
    m = qk.shape[0]
    lam_vec = pl.BlockSpec((1, HEAD_DIM), lambda h, i: (0, 0))
    return pl.pallas_call(
        functools.partial(_attn_kernel, t=t, tk=tk, cq=cq),
        grid=(N_HEADS, m // t),
        in_specs=[
            pl.BlockSpec((t, V_DIM), lambda h, i: (i, h)),
            pl.BlockSpec((m, V_DIM), lambda h, i: (0, N_HEADS + h)),
            pl.BlockSpec((V_DIM, m), lambda h, i: (h, 0)),
            pl.BlockSpec((N_META, V_DIM), lambda h, i: (0, N_HEADS + h)),
            pl.BlockSpec((V_DIM, N_META), lambda h, i: (h, 0)),
            lam_vec, lam_vec, lam_vec, lam_vec,
            pl.BlockSpec((V_DIM, 1), lambda h, i: (0, 0)),
        ],
        out_specs=pl.BlockSpec((t, V_DIM), lambda h, i: (i, h)),
        out_shape=jax.ShapeDtypeStruct((m, ATT_WIDTH), BF16),
        scratch_shapes=[pltpu.VMEM((2 * t, V_DIM), BF16), pltpu.VMEM((4, tk, cq), F32),
                        pltpu.VMEM((1, 2 * t), F32),
                        pltpu.VMEM((1, 2 * t), F32), pltpu.VMEM((V_DIM, 2 * t), F32)],
        compiler_params=_params("parallel", "arbitrary"),
        name="diff_attention",
    )(qk, qk, vt, qk_meta, vt_meta,
      lq1.reshape(1, HEAD_DIM), lk1.reshape(1, HEAD_DIM), lq2.reshape(1, HEAD_DIM), lk2.reshape(1, HEAD_DIM),
      subln.reshape(V_DIM, 1))


def _merge_kernel(h_ref, ya_ref, yc_ref, s_ref, wg0_ref, wg1_ref, wb0_ref, wb1_ref, wo_ref, o_ref, acc_scr):
    j = pl.program_id(1)

    @pl.when(j == 0)
    def _():
        acc_scr[...] = jnp.zeros_like(acc_scr)

    h = h_ref[...]
    gate_a = jax.nn.sigmoid(jnp.dot(h, wg0_ref[...], preferred_element_type=F32))
    gate_c = jax.nn.sigmoid(jnp.dot(h, wg1_ref[...], preferred_element_type=F32))
    z_a = jnp.dot(ya_ref[...], wb0_ref[...], preferred_element_type=F32)
    z_c = jnp.dot(yc_ref[...], wb1_ref[...], preferred_element_type=F32)
    merged = (gate_a * z_a + gate_c * z_c).astype(BF16)
    acc_scr[...] += jnp.dot(merged, wo_ref[...], preferred_element_type=F32)

    @pl.when(j == pl.num_programs(1) - 1)
    def _():
        o_ref[...] = s_ref[...] + acc_scr[...]


def _merge(h, y_att, y_conv, s, w_gate, w_branch, w_out, *, tm=512, tn=512):
    m = h.shape[0]
    n_d = D_MODEL // tn
    return pl.pallas_call(
        _merge_kernel,
        grid=(m // tm, n_d),
        in_specs=[
            pl.BlockSpec((tm, D_MODEL), lambda i, j: (i, 0)),
            pl.BlockSpec((tm, ATT_WIDTH), lambda i, j: (i, 0)),
            pl.BlockSpec((tm, CONV_WIDTH), lambda i, j: (i, 0)),
            pl.BlockSpec((tm, D_MODEL), lambda i, j: (i, 0)),
            pl.BlockSpec((D_MODEL, tn), lambda i, j: (0, j)),
            pl.BlockSpec((D_MODEL, tn), lambda i, j: (0, n_d + j)),
            pl.BlockSpec((None, ATT_WIDTH, tn), lambda i, j: (0, 0, j)),
            pl.BlockSpec((None, CONV_WIDTH, tn), lambda i, j: (1, 0, j)),
            pl.BlockSpec((tn, D_MODEL), lambda i, j: (j, 0)),
        ],
        out_specs=pl.BlockSpec((tm, D_MODEL), lambda i, j: (i, 0)),
        out_shape=jax.ShapeDtypeStruct((m, D_MODEL), F32),
        scratch_shapes=[pltpu.VMEM((tm, D_MODEL), F32)],
        compiler_params=_params("parallel", "arbitrary"),
        name="gated_merge",
    )(h, y_att, y_conv, s, w_gate, w_gate, w_branch, w_branch, w_out)


def kernel(x, meta_tokens, norm_ffn1, ffn1_w_gu, ffn1_w_down, norm_mix, w_in, conv_w, lambda_q1, lambda_k1,
           lambda_q2, lambda_k2, subln, w_branch, w_gate, w_out, norm_ffn2, ffn2_w_gu, ffn2_w_down, norm_final):
    bsz, seq, _ = x.shape
    assert bsz == 1 and meta_tokens.shape == (N_META, D_MODEL)
    xs = x.reshape(seq, D_MODEL)
    w_in = w_in[0].astype(BF16)
    c0 = 3 * ATT_WIDTH
    w_qk, w_vt = w_in[:, :2 * ATT_WIDTH], w_in[:, 2 * ATT_WIDTH:c0].T
    w_b, w_c, w_u = (w_in[:, c0:c0 + CONV_WIDTH], w_in[:, c0 + CONV_WIDTH:c0 + 2 * CONV_WIDTH],
                     w_in[:, c0 + 2 * CONV_WIDTH:])
    w_gu1, w_d1 = ffn1_w_gu[0].astype(BF16), ffn1_w_down[0].astype(BF16)

    def mixer_inputs(tokens, carry_in, tm):
        s1, h = _ffn(tokens, norm_ffn1[0], w_gu1, w_d1, norm_mix[0], final=False, tm=tm)
        qk = _proj(h, w_qk, tm=tm)
        vt = _proj_t(h, w_vt, tm=tm)
        y_conv, tails = _conv(h, w_b, w_c, w_u, conv_w[0], carry_in, tm=tm)
        return s1, h, qk, vt, y_conv, tails

    zero_carry = jnp.zeros((SUBLANES, CONV_WIDTH), F32)
    _, _, qk_meta, vt_meta, _, meta_tail = mixer_inputs(meta_tokens.astype(x.dtype), zero_carry, N_META)
    s1, h, qk, vt, y_conv, _ = mixer_inputs(xs, meta_tail, 512)

    y_att = _attention(qk, vt, qk_meta, vt_meta, lambda_q1[0], lambda_k1[0], lambda_q2[0], lambda_k2[0], subln[0])
    s2 = _merge(h, y_att, y_conv, s1, w_gate[0].astype(BF16), w_branch[0].astype(BF16), w_out[0].astype(BF16))
    out = _ffn(s2, norm_ffn2[0], ffn2_w_gu[0].astype(BF16), ffn2_w_down[0].astype(BF16), norm_final,
               final=True, tm=512)
    return out.reshape(bsz, seq, D_MODEL)
```

```python
import functools
import math

import jax
import jax.numpy as jnp
from jax import lax
from jax.experimental import pallas as pl
from jax.experimental.pallas import tpu as pltpu

D_MODEL = 2048
N_META = 16
HEAD_DIM = 64
V_DIM = 2 * HEAD_DIM
ATT_WIDTH = D_MODEL // 2
N_HEADS = ATT_WIDTH // V_DIM
CONV_WIDTH = D_MODEL // 2
CONV_K = 3
D_FF = ((8 * D_MODEL // 3 + 255) // 256) * 256
RMS_EPS = 1e-6
LAMBDA_INIT = 0.8 - 0.6 * math.exp(-0.3 * 0)
MASKED = -1e30
Q_SCALE = math.log2(math.e) * HEAD_DIM ** -0.5

SUBLANES = 8
MXU_WIDTH = 256
VMEM_LIMIT_BYTES = 56 * 1024 * 1024

BF16 = jnp.bfloat16
F32 = jnp.float32
NT_DIMS = (((1,), (1,)), ((), ()))


def _rms_scale(x):
    return x * lax.rsqrt(jnp.mean(x * x, axis=-1, keepdims=True) + RMS_EPS)


def _params(*semantics):
    return pltpu.CompilerParams(dimension_semantics=semantics, vmem_limit_bytes=VMEM_LIMIT_BYTES)


def _ffn_kernel(x_ref, g_ref, wg_ref, wu_ref, wd_ref, gn_ref, *refs, final):
    if final:
        o_ref, h_scr, acc_scr = refs
    else:
        s_ref, hn_ref, h_scr, acc_scr = refs
    j = pl.program_id(1)

    @pl.when(j == 0)
    def _():
        h_scr[...] = (_rms_scale(x_ref[...]) * g_ref[...]).astype(BF16)
        acc_scr[...] = jnp.zeros_like(acc_scr)

    h = h_scr[...]
    gate = jnp.dot(h, wg_ref[...], preferred_element_type=F32)
    up = jnp.dot(h, wu_ref[...], preferred_element_type=F32)
    act = (gate * jax.nn.sigmoid(gate) * up).astype(BF16)
    acc_scr[...] += jnp.dot(act, wd_ref[...], preferred_element_type=F32)

    @pl.when(j == pl.num_programs(1) - 1)
    def _():
        s = x_ref[...] + 0.5 * acc_scr[...]
        normed = _rms_scale(s) * gn_ref[...]
        if final:
            o_ref[...] = normed
        else:
            s_ref[...] = s
            hn_ref[...] = normed.astype(BF16)


def _ffn(x, g, w_gu, w_down, g_next, *, final, tm, tf=512):
    m = x.shape[0]
    n_f = D_FF // tf
    row = pl.BlockSpec((tm, D_MODEL), lambda i, j: (i, 0))
    vec = pl.BlockSpec((1, D_MODEL), lambda i, j: (0, 0))
    if final:
        out_shape = jax.ShapeDtypeStruct((m, D_MODEL), F32)
        out_specs = row
    else:
        out_shape = (jax.ShapeDtypeStruct((m, D_MODEL), F32), jax.ShapeDtypeStruct((m, D_MODEL), BF16))
        out_specs = (row, row)
    return pl.pallas_call(
        functools.partial(_ffn_kernel, final=final),
        grid=(m // tm, n_f),
        in_specs=[
            row, vec,
            pl.BlockSpec((D_MODEL, tf), lambda i, j: (0, j)),
            pl.BlockSpec((D_MODEL, tf), lambda i, j: (0, n_f + j)),
            pl.BlockSpec((tf, D_MODEL), lambda i, j: (j, 0)),
            vec,
        ],
        out_specs=out_specs,
        out_shape=out_shape,
        scratch_shapes=[pltpu.VMEM((tm, D_MODEL), BF16), pltpu.VMEM((tm, D_MODEL), F32)],
        compiler_params=_params("parallel", "arbitrary"),
        name="ffn_final" if final else "ffn_mix",
    )(x, g.reshape(1, D_MODEL), w_gu, w_gu, w_down, g_next.reshape(1, D_MODEL))


def _mixer_proj_kernel(h_ref, w_ref, wvt_ref, cw_ref, cin_ref, qk_ref, vt_ref, y_ref, cout_ref, carry_scr):
    i = pl.program_id(0)
    tm = h_ref.shape[0]

    @pl.when(i == 0)
    def _():
        carry_scr[...] = cin_ref[...]

    h = h_ref[...]

    def proj(first_col, width):
        return jnp.dot(h, w_ref[:, first_col:first_col + width], preferred_element_type=F32)

    qk_ref[:, 0:ATT_WIDTH] = (proj(0, ATT_WIDTH) * Q_SCALE).astype(BF16)
    qk_ref[:, ATT_WIDTH:2 * ATT_WIDTH] = proj(ATT_WIDTH, ATT_WIDTH).astype(BF16)
    vt_ref[...] = lax.dot_general(wvt_ref[...], h, NT_DIMS, preferred_element_type=F32).astype(BF16)

    c0 = 3 * ATT_WIDTH
    b_gate = proj(c0, CONV_WIDTH)
    v = proj(c0 + CONV_WIDTH, CONV_WIDTH) * proj(c0 + 2 * CONV_WIDTH, CONV_WIDTH)
    w0, w1, w2 = cw_ref[0:1, :], cw_ref[1:2, :], cw_ref[2:3, :]

    y = w0 * pltpu.roll(v, 2, axis=0) + w1 * pltpu.roll(v, 1, axis=0) + w2 * v
    y_ref[...] = (b_gate * y).astype(BF16)

    prev = carry_scr[...]
    head = v[0:SUBLANES, :]
    rows = lax.broadcasted_iota(jnp.int32, head.shape, 0)
    v1 = jnp.where(rows >= 1, pltpu.roll(head, 1, axis=0), pltpu.roll(prev, 1, axis=0))
    v2 = jnp.where(rows >= 2, pltpu.roll(head, 2, axis=0), pltpu.roll(prev, 2, axis=0))
    y_head = w0 * v2 + w1 * v1 + w2 * head
    y_ref[0:SUBLANES, :] = (b_gate[0:SUBLANES, :] * y_head).astype(BF16)

    tail = v[tm - SUBLANES:tm, :]
    carry_scr[...] = tail
    cout_ref[...] = tail


def _mixer_proj(h, w_in, w_vt, conv_w, carry_in, *, tm):
    m = h.shape[0]
    whole = functools.partial(pl.BlockSpec, index_map=lambda i: (0, 0), pipeline_mode=pl.Buffered(1))
    return pl.pallas_call(
        _mixer_proj_kernel,
        grid=(m // tm,),
        in_specs=[pl.BlockSpec((tm, D_MODEL), lambda i: (i, 0)),
                  whole(w_in.shape), whole(w_vt.shape), whole((CONV_K, CONV_WIDTH)), whole((SUBLANES, CONV_WIDTH))],
        out_specs=(pl.BlockSpec((tm, 2 * ATT_WIDTH), lambda i: (i, 0)),
                   pl.BlockSpec((ATT_WIDTH, tm), lambda i: (0, i)),
                   pl.BlockSpec((tm, CONV_WIDTH), lambda i: (i, 0)),
                   pl.BlockSpec((SUBLANES, CONV_WIDTH), lambda i: (i, 0))),
        out_shape=(jax.ShapeDtypeStruct((m, 2 * ATT_WIDTH), BF16),
                   jax.ShapeDtypeStruct((ATT_WIDTH, m), BF16),
                   jax.ShapeDtypeStruct((m, CONV_WIDTH), BF16),
                   jax.ShapeDtypeStruct((m // tm * SUBLANES, CONV_WIDTH), F32)),
        scratch_shapes=[pltpu.VMEM((SUBLANES, CONV_WIDTH), F32)],
        compiler_params=_params("arbitrary"),
        name="mixer_proj",
    )(h, w_in, w_vt, conv_w, carry_in)


def _attn_kernel(q_ref, k_ref, vt_ref, km_ref, vtm_ref, lq1_ref, lk1_ref, lq2_ref, lk2_ref, g_ref,
                 o_ref, q2_scr, s_scr, m_scr, l_scr, acc_scr, *, t, tk, cq, tiles_per_iter):
    i = pl.program_id(1)
    n_chunks = 2 * t // cq

    q = q_ref[...]
    lane = lax.broadcasted_iota(jnp.int32, q.shape, 1)
    zero = jnp.zeros_like(q)
    q2_scr[0:t, :] = jnp.where(lane < HEAD_DIM, q, zero)
    q2_scr[t:2 * t, :] = jnp.where(lane >= HEAD_DIM, q, zero)

    def scores(k, c):
        return lax.dot_general(k, q2_scr[c * cq:(c + 1) * cq, :], NT_DIMS, preferred_element_type=F32)

    def keys_of(base, item):
        if item[0] is None:
            return km_ref[...], vtm_ref[...]
        return k_ref[pl.ds(base + item[0], tk), :], vt_ref[:, pl.ds(base + item[0], tk)]

    def chunk_update(c, s_ref, vt, key_offset):
        cols = slice(c * cq, (c + 1) * cq)

        def load_scores():
            s = s_ref[...]
            if key_offset is None:
                return s
            key = key_offset + lax.broadcasted_iota(jnp.int32, s.shape, 0)
            query = (c * cq) % t + lax.broadcasted_iota(jnp.int32, s.shape, 1)
            return jnp.where(key <= query, s, MASKED)

        m_old = m_scr[:, cols]
        m_new = jnp.maximum(m_old, jnp.max(load_scores(), axis=0, keepdims=True))
        alpha = jnp.exp2(m_old - m_new)
        p = jnp.exp2(load_scores() - m_new)
        m_scr[:, cols] = m_new
        l_scr[:, cols] = alpha * l_scr[:, cols] + jnp.sum(p, axis=0, keepdims=True)
        acc_scr[:, cols] = alpha * acc_scr[:, cols] + jnp.dot(vt, p.astype(BF16), preferred_element_type=F32)

    n_slots = s_scr.shape[0]
    ahead = n_slots // 2

    def issue_scores(base, item, slot):
        k = keys_of(base, item)[0]
        s_scr[slot, 0:k.shape[0], :] = scores(k, item[1])

    def run(base, items, next_items):
        assert len(items) % n_slots == 0 or not next_items
        upcoming = items + next_items
        for n, item in enumerate(items):
            if n + ahead < len(upcoming):
                issue_scores(base, upcoming[n + ahead], (n + ahead) % n_slots)
            vt = keys_of(base, item)[1]
            chunk_update(item[1], s_scr.at[n % n_slots, 0:vt.shape[1], :], vt, item[2])

    visible_items = [(d * tk, c, None) for d in range(t // tk) for c in range(n_chunks)]
    diagonal_items = []
    for d in range(t // tk):
        for c in range(n_chunks):
            first_query = (c * cq) % t
            if d * tk <= first_query + cq - 1:
                fully_visible = d * tk + tk - 1 <= first_query
                diagonal_items.append((d * tk, c, None if fully_visible else d * tk))
    diagonal_items += [(None, c, None) for c in range(n_chunks)]

    m_scr[...] = jnp.full_like(m_scr, MASKED)
    l_scr[...] = jnp.zeros_like(l_scr)
    acc_scr[...] = jnp.zeros_like(acc_scr)
    assert [item[:2] for item in diagonal_items[:ahead]] == [item[:2] for item in visible_items[:ahead]]
    for n in range(ahead):
        issue_scores(0, visible_items[n], n)

    def shifted(items, tiles):
        return [(tiles * t + rel, c, key_offset) for rel, c, key_offset in items]

    def body(g, carry):
        group = [item for n in range(tiles_per_iter) for item in shifted(visible_items, n)]
        run(pl.multiple_of(g * tiles_per_iter * t, tiles_per_iter * t), group,
            shifted(visible_items[:ahead], tiles_per_iter))
        return carry

    lax.fori_loop(0, i // tiles_per_iter, body, 0)
    for left in range(1, tiles_per_iter):
        @pl.when(i % tiles_per_iter == left)
        def _():
            group = [item for n in range(left) for item in shifted(visible_items, n)]
            run(pl.multiple_of((i - left) * t, t), group, shifted(visible_items[:ahead], left))

    run(pl.multiple_of(i * t, t), diagonal_items, [])

    lam = (jnp.exp(jnp.sum(lq1_ref[...] * lk1_ref[...], axis=-1, keepdims=True))
           - jnp.exp(jnp.sum(lq2_ref[...] * lk2_ref[...], axis=-1, keepdims=True)) + LAMBDA_INIT)
    o = acc_scr[...] / l_scr[...]
    out = o[:, 0:t] - lam * o[:, t:2 * t]
    normed = out * lax.rsqrt(jnp.mean(out * out, axis=0, keepdims=True) + RMS_EPS)
    o_ref[...] = ((normed * g_ref[...]) * (1.0 - LAMBDA_INIT)).T.astype(BF16)


def _attention(qk, vt, qk_meta, vt_meta, lq1, lk1, lq2, lk2, subln, *, t=2048, tk=512, cq=2 * MXU_WIDTH,
               tiles_per_iter=1):
    m = qk.shape[0]
    lam_vec = pl.BlockSpec((1, HEAD_DIM), lambda h, i: (0, 0))
    return pl.pallas_call(
        functools.partial(_attn_kernel, t=t, tk=tk, cq=cq, tiles_per_iter=tiles_per_iter),
        grid=(N_HEADS, m // t),
        in_specs=[
            pl.BlockSpec((t, V_DIM), lambda h, i: (i, h)),
            pl.BlockSpec((m, V_DIM), lambda h, i: (0, N_HEADS + h)),
            pl.BlockSpec((V_DIM, m), lambda h, i: (h, 0)),
            pl.BlockSpec((N_META, V_DIM), lambda h, i: (0, N_HEADS + h)),
            pl.BlockSpec((V_DIM, N_META), lambda h, i: (h, 0)),
            lam_vec, lam_vec, lam_vec, lam_vec,
            pl.BlockSpec((V_DIM, 1), lambda h, i: (0, 0)),
        ],
        out_specs=pl.BlockSpec((t, V_DIM), lambda h, i: (i, h)),
        out_shape=jax.ShapeDtypeStruct((m, ATT_WIDTH), BF16),
        scratch_shapes=[pltpu.VMEM((2 * t, V_DIM), BF16), pltpu.VMEM((4, tk, cq), F32),
                        pltpu.VMEM((1, 2 * t), F32),
                        pltpu.VMEM((1, 2 * t), F32), pltpu.VMEM((V_DIM, 2 * t), F32)],
        compiler_params=_params("parallel", "arbitrary"),
        name="diff_attention",
    )(qk, qk, vt, qk_meta, vt_meta,
      lq1.reshape(1, HEAD_DIM), lk1.reshape(1, HEAD_DIM), lq2.reshape(1, HEAD_DIM), lk2.reshape(1, HEAD_DIM),
      subln.reshape(V_DIM, 1))


def _merge_kernel(h_ref, ya_ref, yc_ref, s_ref, wg0_ref, wg1_ref, wb0_ref, wb1_ref, wo_ref, o_ref, acc_scr):
    j = pl.program_id(1)

    @pl.when(j == 0)
    def _():
        acc_scr[...] = jnp.zeros_like(acc_scr)

    h = h_ref[...]
    gate_a = jax.nn.sigmoid(jnp.dot(h, wg0_ref[...], preferred_element_type=F32))
    gate_c = jax.nn.sigmoid(jnp.dot(h, wg1_ref[...], preferred_element_type=F32))
    z_a = jnp.dot(ya_ref[...], wb0_ref[...], preferred_element_type=F32)
    z_c = jnp.dot(yc_ref[...], wb1_ref[...], preferred_element_type=F32)
    merged = (gate_a * z_a + gate_c * z_c).astype(BF16)
    acc_scr[...] += jnp.dot(merged, wo_ref[...], preferred_element_type=F32)

    @pl.when(j == pl.num_programs(1) - 1)
    def _():
        o_ref[...] = s_ref[...] + acc_scr[...]


def _merge(h, y_att, y_conv, s, w_gate, w_branch, w_out, *, tm=512, tn=512):
    m = h.shape[0]
    n_d = D_MODEL // tn
    return pl.pallas_call(
        _merge_kernel,
        grid=(m // tm, n_d),
        in_specs=[
            pl.BlockSpec((tm, D_MODEL), lambda i, j: (i, 0)),
            pl.BlockSpec((tm, ATT_WIDTH), lambda i, j: (i, 0)),
            pl.BlockSpec((tm, CONV_WIDTH), lambda i, j: (i, 0)),
            pl.BlockSpec((tm, D_MODEL), lambda i, j: (i, 0)),
            pl.BlockSpec((D_MODEL, tn), lambda i, j: (0, j)),
            pl.BlockSpec((D_MODEL, tn), lambda i, j: (0, n_d + j)),
            pl.BlockSpec((None, ATT_WIDTH, tn), lambda i, j: (0, 0, j)),
            pl.BlockSpec((None, CONV_WIDTH, tn), lambda i, j: (1, 0, j)),
            pl.BlockSpec((tn, D_MODEL), lambda i, j: (j, 0)),
        ],
        out_specs=pl.BlockSpec((tm, D_MODEL), lambda i, j: (i, 0)),
        out_shape=jax.ShapeDtypeStruct((m, D_MODEL), F32),
        scratch_shapes=[pltpu.VMEM((tm, D_MODEL), F32)],
        compiler_params=_params("parallel", "arbitrary"),
        name="gated_merge",
    )(h, y_att, y_conv, s, w_gate, w_gate, w_branch, w_branch, w_out)


def kernel(x, meta_tokens, norm_ffn1, ffn1_w_gu, ffn1_w_down, norm_mix, w_in, conv_w, lambda_q1, lambda_k1,
           lambda_q2, lambda_k2, subln, w_branch, w_gate, w_out, norm_ffn2, ffn2_w_gu, ffn2_w_down, norm_final):
    bsz, seq, _ = x.shape
    assert bsz == 1 and meta_tokens.shape == (N_META, D_MODEL)
    xs = x.reshape(seq, D_MODEL)
    w_in = w_in[0].astype(BF16)
    w_vt = w_in[:, 2 * ATT_WIDTH:3 * ATT_WIDTH].T
    w_gu1, w_d1 = ffn1_w_gu[0].astype(BF16), ffn1_w_down[0].astype(BF16)

    def mixer_inputs(tokens, carry_in, tm):
        s1, h = _ffn(tokens, norm_ffn1[0], w_gu1, w_d1, norm_mix[0], final=False, tm=tm)
        qk, vt, y_conv, tails = _mixer_proj(h, w_in, w_vt, conv_w[0], carry_in, tm=tm)
        return s1, h, qk, vt, y_conv, tails

    zero_carry = jnp.zeros((SUBLANES, CONV_WIDTH), F32)
    _, _, qk_meta, vt_meta, _, meta_tail = mixer_inputs(meta_tokens.astype(x.dtype), zero_carry, N_META)
    s1, h, qk, vt, y_conv, _ = mixer_inputs(xs, meta_tail, 512)

    y_att = _attention(qk, vt, qk_meta, vt_meta, lambda_q1[0], lambda_k1[0], lambda_q2[0], lambda_k2[0], subln[0])
    s2 = _merge(h, y_att, y_conv, s1, w_gate[0].astype(BF16), w_branch[0].astype(BF16), w_out[0].astype(BF16))
    out = _ffn(s2, norm_ffn2[0], ffn2_w_gu[0].astype(BF16), ffn2_w_down[0].astype(BF16), norm_final,
               final=True, tm=512)
    return out.reshape(bsz, seq, D_MODEL)
```

```python
import functools
import math

import jax
import jax.numpy as jnp
from jax import lax
from jax.experimental import pallas as pl
from jax.experimental.pallas import tpu as pltpu

D_MODEL = 2048
N_META = 16
HEAD_DIM = 64
V_DIM = 2 * HEAD_DIM
ATT_WIDTH = D_MODEL // 2
N_HEADS = ATT_WIDTH // V_DIM
CONV_WIDTH = D_MODEL // 2
CONV_K = 3
N_BRANCH = 2
D_FF = ((8 * D_MODEL // 3 + 255) // 256) * 256
RMS_EPS = 1e-6
LAMBDA_INIT = 0.8 - 0.6 * math.exp(-0.3 * 0)
MASKED = -1e30
Q_SCALE = math.log2(math.e) * HEAD_DIM ** -0.5

SUBLANES = 8
BF16_ROWS = 16
MXU_WIDTH = 256
VMEM_LIMIT_BYTES = 56 * 1024 * 1024

BF16 = jnp.bfloat16
F32 = jnp.float32
NT_DIMS = (((1,), (1,)), ((), ()))


def _rms_scale(x):
    return x * lax.rsqrt(jnp.mean(x * x, axis=-1, keepdims=True) + RMS_EPS)


def _params(*semantics):
    return pltpu.CompilerParams(dimension_semantics=semantics, vmem_limit_bytes=VMEM_LIMIT_BYTES)


def _cast_plumbing(weights, n_blocks, index_map):
    specs, shapes = [], []
    for w in weights:
        rows, cols = w.shape
        assert rows % (n_blocks * BF16_ROWS) == 0, (w.shape, n_blocks)
        specs.append(pl.BlockSpec((rows // n_blocks, cols), index_map))
        shapes.append(jax.ShapeDtypeStruct(w.shape, BF16))
    return specs, list(specs), shapes


def _cast_blocks(src_refs, dst_refs):
    for src, dst in zip(src_refs, dst_refs):
        dst[...] = src[...].astype(BF16)


def _ffn_kernel(x_ref, g_ref, wg_ref, wu_ref, wd_ref, gn_ref, *refs, final, n_cast):
    cast_in, refs = refs[:n_cast], refs[n_cast:]
    n_out = 1 if final else 2
    outs, cast_out, (h_scr, acc_scr) = refs[:n_out], refs[n_out:n_out + n_cast], refs[n_out + n_cast:]
    j = pl.program_id(1)

    @pl.when(j == 0)
    def _():
        h_scr[...] = (_rms_scale(x_ref[...]) * g_ref[...]).astype(BF16)
        acc_scr[...] = jnp.zeros_like(acc_scr)
        _cast_blocks(cast_in, cast_out)

    h = h_scr[...]
    gate = jnp.dot(h, wg_ref[...], preferred_element_type=F32)
    up = jnp.dot(h, wu_ref[...], preferred_element_type=F32)
    act = (gate * jax.nn.sigmoid(gate) * up).astype(BF16)
    acc_scr[...] += jnp.dot(act, wd_ref[...], preferred_element_type=F32)

    @pl.when(j == pl.num_programs(1) - 1)
    def _():
        s = x_ref[...] + 0.5 * acc_scr[...]
        normed = _rms_scale(s) * gn_ref[...]
        if final:
            outs[0][...] = normed
        else:
            outs[0][...] = s
            outs[1][...] = normed.astype(BF16)


def _ffn(x, g, w_gu, w_down, g_next, *, final, tm, tf=512, cast=()):
    m = x.shape[0]
    n_f = D_FF // tf
    row = pl.BlockSpec((tm, D_MODEL), lambda i, j: (i, 0))
    vec = pl.BlockSpec((1, D_MODEL), lambda i, j: (0, 0))
    out_shapes = [jax.ShapeDtypeStruct((m, D_MODEL), F32)]
    if not final:
        out_shapes.append(jax.ShapeDtypeStruct((m, D_MODEL), BF16))
    cast_in, cast_out, cast_shapes = _cast_plumbing(cast, m // tm, lambda i, j: (i, 0))
    return pl.pallas_call(
        functools.partial(_ffn_kernel, final=final, n_cast=len(cast)),
        grid=(m // tm, n_f),
        in_specs=[
            row, vec,
            pl.BlockSpec((D_MODEL, tf), lambda i, j: (0, j)),
            pl.BlockSpec((D_MODEL, tf), lambda i, j: (0, n_f + j)),
            pl.BlockSpec((tf, D_MODEL), lambda i, j: (j, 0)),
            vec,
        ] + cast_in,
        out_specs=[row] * len(out_shapes) + cast_out,
        out_shape=out_shapes + cast_shapes,
        scratch_shapes=[pltpu.VMEM((tm, D_MODEL), BF16), pltpu.VMEM((tm, D_MODEL), F32)],
        compiler_params=_params("parallel", "arbitrary"),
        name="ffn_final" if final else "ffn_mix",
    )(x, g.reshape(1, D_MODEL), w_gu, w_gu, w_down, g_next.reshape(1, D_MODEL), *cast)


def _mixer_proj_kernel(h_ref, w_ref, cw_ref, cin_ref, qk_ref, v_ref, y_ref, cout_ref, carry_scr, *, transpose_v):
    i = pl.program_id(0)
    tm = h_ref.shape[0]

    @pl.when(i == 0)
    def _():
        carry_scr[...] = cin_ref[...]

    h = h_ref[...]

    def proj(first_col, width):
        return jnp.dot(h, w_ref[:, first_col:first_col + width], preferred_element_type=F32)

    qk_ref[:, 0:ATT_WIDTH] = (proj(0, ATT_WIDTH) * Q_SCALE).astype(BF16)
    qk_ref[:, ATT_WIDTH:2 * ATT_WIDTH] = proj(ATT_WIDTH, ATT_WIDTH).astype(BF16)
    v_proj = proj(2 * ATT_WIDTH, ATT_WIDTH)
    v_ref[...] = (v_proj.T if transpose_v else v_proj).astype(BF16)

    c0 = 3 * ATT_WIDTH
    b_gate = proj(c0, CONV_WIDTH)
    v = proj(c0 + CONV_WIDTH, CONV_WIDTH) * proj(c0 + 2 * CONV_WIDTH, CONV_WIDTH)
    w0, w1, w2 = cw_ref[0:1, :], cw_ref[1:2, :], cw_ref[2:3, :]

    y = w0 * pltpu.roll(v, 2, axis=0) + w1 * pltpu.roll(v, 1, axis=0) + w2 * v
    y_ref[...] = (b_gate * y).astype(BF16)

    prev = carry_scr[...]
    head = v[0:SUBLANES, :]
    rows = lax.broadcasted_iota(jnp.int32, head.shape, 0)
    v1 = jnp.where(rows >= 1, pltpu.roll(head, 1, axis=0), pltpu.roll(prev, 1, axis=0))
    v2 = jnp.where(rows >= 2, pltpu.roll(head, 2, axis=0), pltpu.roll(prev, 2, axis=0))
    y_head = w0 * v2 + w1 * v1 + w2 * head
    y_ref[0:SUBLANES, :] = (b_gate[0:SUBLANES, :] * y_head).astype(BF16)

    tail = v[tm - SUBLANES:tm, :]
    carry_scr[...] = tail
    cout_ref[...] = tail


def _mixer_proj(h, w_in, conv_w, carry_in, *, tm, transpose_v):
    m = h.shape[0]
    whole = functools.partial(pl.BlockSpec, index_map=lambda i: (0, 0), pipeline_mode=pl.Buffered(1))
    if transpose_v:
        v_spec, v_shape = pl.BlockSpec((ATT_WIDTH, tm), lambda i: (0, i)), (ATT_WIDTH, m)
    else:
        v_spec, v_shape = pl.BlockSpec((tm, ATT_WIDTH), lambda i: (i, 0)), (m, ATT_WIDTH)
    return pl.pallas_call(
        functools.partial(_mixer_proj_kernel, transpose_v=transpose_v),
        grid=(m // tm,),
        in_specs=[pl.BlockSpec((tm, D_MODEL), lambda i: (i, 0)),
                  whole(w_in.shape), whole((CONV_K, CONV_WIDTH)), whole((SUBLANES, CONV_WIDTH))],
        out_specs=(pl.BlockSpec((tm, 2 * ATT_WIDTH), lambda i: (i, 0)),
                   v_spec,
                   pl.BlockSpec((tm, CONV_WIDTH), lambda i: (i, 0)),
                   pl.BlockSpec((SUBLANES, CONV_WIDTH), lambda i: (i, 0))),
        out_shape=(jax.ShapeDtypeStruct((m, 2 * ATT_WIDTH), BF16),
                   jax.ShapeDtypeStruct(v_shape, BF16),
                   jax.ShapeDtypeStruct((m, CONV_WIDTH), BF16),
                   jax.ShapeDtypeStruct((m // tm * SUBLANES, CONV_WIDTH), F32)),
        scratch_shapes=[pltpu.VMEM((SUBLANES, CONV_WIDTH), F32)],
        compiler_params=_params("arbitrary"),
        name="mixer_proj",
    )(h, w_in, conv_w, carry_in)


def _attn_kernel(q_ref, k_ref, vt_ref, km_ref, vtm_ref, lq1_ref, lk1_ref, lq2_ref, lk2_ref, g_ref, *refs,
                 t, tk, cq, tiles_per_iter, n_cast, steps_per_cast):
    cast_in, o_ref, cast_out = refs[:n_cast], refs[n_cast], refs[n_cast + 1:2 * n_cast + 1]
    q2_scr, s_scr, m_scr, l_scr, acc_scr = refs[2 * n_cast + 1:]
    i = pl.program_id(1)
    n_chunks = 2 * t // cq

    @pl.when(i % steps_per_cast == 0)
    def _():
        _cast_blocks(cast_in, cast_out)

    q = q_ref[...]
    lane = lax.broadcasted_iota(jnp.int32, q.shape, 1)
    zero = jnp.zeros_like(q)
    q2_scr[0:t, :] = jnp.where(lane < HEAD_DIM, q, zero)
    q2_scr[t:2 * t, :] = jnp.where(lane >= HEAD_DIM, q, zero)

    def scores(k, c):
        return lax.dot_general(k, q2_scr[c * cq:(c + 1) * cq, :], NT_DIMS, preferred_element_type=F32)

    def keys_of(base, item):
        if item[0] is None:
            return km_ref[...], vtm_ref[...]
        return k_ref[pl.ds(base + item[0], tk), :], vt_ref[:, pl.ds(base + item[0], tk)]

    def chunk_update(c, s_ref, vt, key_offset):
        cols = slice(c * cq, (c + 1) * cq)

        def load_scores():
            s = s_ref[...]
            if key_offset is None:
                return s
            key = key_offset + lax.broadcasted_iota(jnp.int32, s.shape, 0)
            query = (c * cq) % t + lax.broadcasted_iota(jnp.int32, s.shape, 1)
            return jnp.where(key <= query, s, MASKED)

        m_old = m_scr[:, cols]
        m_new = jnp.maximum(m_old, jnp.max(load_scores(), axis=0, keepdims=True))
        alpha = jnp.exp2(m_old - m_new)
        p = jnp.exp2(load_scores() - m_new)
        m_scr[:, cols] = m_new
        l_scr[:, cols] = alpha * l_scr[:, cols] + jnp.sum(p, axis=0, keepdims=True)
        acc_scr[:, cols] = alpha * acc_scr[:, cols] + jnp.dot(vt, p.astype(BF16), preferred_element_type=F32)

    n_slots = s_scr.shape[0]
    ahead = n_slots // 2

    def issue_scores(base, item, slot):
        k = keys_of(base, item)[0]
        s_scr[slot, 0:k.shape[0], :] = scores(k, item[1])

    def run(base, items, next_items):
        assert len(items) % n_slots == 0 or not next_items
        upcoming = items + next_items
        for n, item in enumerate(items):
            if n + ahead < len(upcoming):
                issue_scores(base, upcoming[n + ahead], (n + ahead) % n_slots)
            vt = keys_of(base, item)[1]
            chunk_update(item[1], s_scr.at[n % n_slots, 0:vt.shape[1], :], vt, item[2])

    visible_items = [(d * tk, c, None) for d in range(t // tk) for c in range(n_chunks)]
    diagonal_items = []
    for d in range(t // tk):
        for c in range(n_chunks):
            first_query = (c * cq) % t
            if d * tk <= first_query + cq - 1:
                fully_visible = d * tk + tk - 1 <= first_query
                diagonal_items.append((d * tk, c, None if fully_visible else d * tk))
    diagonal_items += [(None, c, None) for c in range(n_chunks)]

    m_scr[...] = jnp.full_like(m_scr, MASKED)
    l_scr[...] = jnp.zeros_like(l_scr)
    acc_scr[...] = jnp.zeros_like(acc_scr)
    assert [item[:2] for item in diagonal_items[:ahead]] == [item[:2] for item in visible_items[:ahead]]
    for n in range(ahead):
        issue_scores(0, visible_items[n], n)

    def shifted(items, tiles):
        return [(tiles * t + rel, c, key_offset) for rel, c, key_offset in items]

    def body(g, carry):
        group = [item for n in range(tiles_per_iter) for item in shifted(visible_items, n)]
        run(pl.multiple_of(g * tiles_per_iter * t, tiles_per_iter * t), group,
            shifted(visible_items[:ahead], tiles_per_iter))
        return carry

    lax.fori_loop(0, i // tiles_per_iter, body, 0)
    for left in range(1, tiles_per_iter):
        @pl.when(i % tiles_per_iter == left)
        def _():
            group = [item for n in range(left) for item in shifted(visible_items, n)]
            run(pl.multiple_of((i - left) * t, t), group, shifted(visible_items[:ahead], left))

    run(pl.multiple_of(i * t, t), diagonal_items, [])

    lam = (jnp.exp(jnp.sum(lq1_ref[...] * lk1_ref[...], axis=-1, keepdims=True))
           - jnp.exp(jnp.sum(lq2_ref[...] * lk2_ref[...], axis=-1, keepdims=True)) + LAMBDA_INIT)
    o = acc_scr[...] / l_scr[...]
    out = o[:, 0:t] - lam * o[:, t:2 * t]
    normed = out * lax.rsqrt(jnp.mean(out * out, axis=0, keepdims=True) + RMS_EPS)
    o_ref[...] = ((normed * g_ref[...]) * (1.0 - LAMBDA_INIT)).T.astype(BF16)


def _attention(qk, vt, qk_meta, vt_meta, lq1, lk1, lq2, lk2, subln, *, t=2048, tk=512, cq=2 * MXU_WIDTH,
               tiles_per_iter=1, cast=(), steps_per_cast=2):
    m = qk.shape[0]
    lam_vec = pl.BlockSpec((1, HEAD_DIM), lambda h, i: (0, 0))
    casts_per_head = m // t // steps_per_cast
    cast_in, cast_out, cast_shapes = _cast_plumbing(
        cast, N_HEADS * casts_per_head, lambda h, i: (h * casts_per_head + i // steps_per_cast, 0))
    return pl.pallas_call(
        functools.partial(_attn_kernel, t=t, tk=tk, cq=cq, tiles_per_iter=tiles_per_iter, n_cast=len(cast),
                          steps_per_cast=steps_per_cast),
        grid=(N_HEADS, m // t),
        in_specs=[
            pl.BlockSpec((t, V_DIM), lambda h, i: (i, h)),
            pl.BlockSpec((m, V_DIM), lambda h, i: (0, N_HEADS + h)),
            pl.BlockSpec((V_DIM, m), lambda h, i: (h, 0)),
            pl.BlockSpec((N_META, V_DIM), lambda h, i: (0, N_HEADS + h)),
            pl.BlockSpec((V_DIM, N_META), lambda h, i: (h, 0)),
            lam_vec, lam_vec, lam_vec, lam_vec,
            pl.BlockSpec((V_DIM, 1), lambda h, i: (0, 0)),
        ] + cast_in,
        out_specs=[pl.BlockSpec((t, V_DIM), lambda h, i: (i, h))] + cast_out,
        out_shape=[jax.ShapeDtypeStruct((m, ATT_WIDTH), BF16)] + cast_shapes,
        scratch_shapes=[pltpu.VMEM((2 * t, V_DIM), BF16), pltpu.VMEM((4, tk, cq), F32),
                        pltpu.VMEM((1, 2 * t), F32),
                        pltpu.VMEM((1, 2 * t), F32), pltpu.VMEM((V_DIM, 2 * t), F32)],
        compiler_params=_params("parallel", "arbitrary"),
        name="diff_attention",
    )(qk, qk, vt, qk_meta, vt_meta,
      lq1.reshape(1, HEAD_DIM), lk1.reshape(1, HEAD_DIM), lq2.reshape(1, HEAD_DIM), lk2.reshape(1, HEAD_DIM),
      subln.reshape(V_DIM, 1), *cast)


def _merge_kernel(h_ref, ya_ref, yc_ref, s_ref, wg0_ref, wg1_ref, wb0_ref, wb1_ref, wo_ref, o_ref, acc_scr):
    j = pl.program_id(1)

    @pl.when(j == 0)
    def _():
        acc_scr[...] = jnp.zeros_like(acc_scr)

    h = h_ref[...]
    gate_a = jax.nn.sigmoid(jnp.dot(h, wg0_ref[...], preferred_element_type=F32))
    gate_c = jax.nn.sigmoid(jnp.dot(h, wg1_ref[...], preferred_element_type=F32))
    z_a = jnp.dot(ya_ref[...], wb0_ref[...], preferred_element_type=F32)
    z_c = jnp.dot(yc_ref[...], wb1_ref[...], preferred_element_type=F32)
    merged = (gate_a * z_a + gate_c * z_c).astype(BF16)
    acc_scr[...] += jnp.dot(merged, wo_ref[...], preferred_element_type=F32)

    @pl.when(j == pl.num_programs(1) - 1)
    def _():
        o_ref[...] = s_ref[...] + acc_scr[...]


def _merge(h, y_att, y_conv, s, w_gate, w_branch, w_out, *, tm=512, tn=512):
    m = h.shape[0]
    n_d = D_MODEL // tn
    return pl.pallas_call(
        _merge_kernel,
        grid=(m // tm, n_d),
        in_specs=[
            pl.BlockSpec((tm, D_MODEL), lambda i, j: (i, 0)),
            pl.BlockSpec((tm, ATT_WIDTH), lambda i, j: (i, 0)),
            pl.BlockSpec((tm, CONV_WIDTH), lambda i, j: (i, 0)),
            pl.BlockSpec((tm, D_MODEL), lambda i, j: (i, 0)),
            pl.BlockSpec((D_MODEL, tn), lambda i, j: (0, j)),
            pl.BlockSpec((D_MODEL, tn), lambda i, j: (0, n_d + j)),
            pl.BlockSpec((None, ATT_WIDTH, tn), lambda i, j: (0, 0, j)),
            pl.BlockSpec((None, CONV_WIDTH, tn), lambda i, j: (1, 0, j)),
            pl.BlockSpec((tn, D_MODEL), lambda i, j: (j, 0)),
        ],
        out_specs=pl.BlockSpec((tm, D_MODEL), lambda i, j: (i, 0)),
        out_shape=jax.ShapeDtypeStruct((m, D_MODEL), F32),
        scratch_shapes=[pltpu.VMEM((tm, D_MODEL), F32)],
        compiler_params=_params("parallel", "arbitrary"),
        name="gated_merge",
    )(h, y_att, y_conv, s, w_gate, w_gate, w_branch, w_branch, w_out)


def kernel(x, meta_tokens, norm_ffn1, ffn1_w_gu, ffn1_w_down, norm_mix, w_in, conv_w, lambda_q1, lambda_k1,
           lambda_q2, lambda_k2, subln, w_branch, w_gate, w_out, norm_ffn2, ffn2_w_gu, ffn2_w_down, norm_final):
    bsz, seq, _ = x.shape
    assert bsz == 1 and meta_tokens.shape == (N_META, D_MODEL)
    xs = x.reshape(seq, D_MODEL)
    w_gu1, w_d1 = ffn1_w_gu[0].astype(BF16), ffn1_w_down[0].astype(BF16)
    s1, h, w_in_bf = _ffn(xs, norm_ffn1[0], w_gu1, w_d1, norm_mix[0], final=False, tm=512, cast=(w_in[0],))

    _, h_meta = _ffn(meta_tokens.astype(x.dtype), norm_ffn1[0], w_gu1, w_d1, norm_mix[0], final=False, tm=N_META)
    qk_meta, v_meta, _, meta_tail = _mixer_proj(h_meta, w_in_bf, conv_w[0], jnp.zeros((SUBLANES, CONV_WIDTH), F32),
                                                tm=N_META, transpose_v=False)
    qk, vt, y_conv, _ = _mixer_proj(h, w_in_bf, conv_w[0], meta_tail, tm=512, transpose_v=True)

    later_weights = (w_gate[0], w_branch[0].reshape(N_BRANCH * ATT_WIDTH, D_MODEL), w_out[0],
                     ffn2_w_gu[0], ffn2_w_down[0])
    y_att, w_gate_bf, w_branch_bf, w_out_bf, w_gu2, w_d2 = _attention(
        qk, vt, qk_meta, v_meta.T, lambda_q1[0], lambda_k1[0], lambda_q2[0], lambda_k2[0], subln[0],
        cast=later_weights)
    s2 = _merge(h, y_att, y_conv, s1, w_gate_bf, w_branch_bf.reshape(N_BRANCH, ATT_WIDTH, D_MODEL), w_out_bf)
    out, = _ffn(s2, norm_ffn2[0], w_gu2, w_d2, norm_final, final=True, tm=512)
    return out.reshape(bsz, seq, D_MODEL)
```

```python
import functools
import math

import jax
import jax.numpy as jnp
from jax import lax
from jax.experimental import pallas as pl
from jax.experimental.pallas import tpu as pltpu

D_MODEL = 2048
N_META = 16
HEAD_DIM = 64
V_DIM = 2 * HEAD_DIM
ATT_WIDTH = D_MODEL // 2
N_HEADS = ATT_WIDTH // V_DIM
CONV_WIDTH = D_MODEL // 2
CONV_K = 3
N_BRANCH = 2
D_FF = ((8 * D_MODEL // 3 + 255) // 256) * 256
RMS_EPS = 1e-6
LAMBDA_INIT = 0.8 - 0.6 * math.exp(-0.3 * 0)
MASKED = -1e30
Q_SCALE = math.log2(math.e) * HEAD_DIM ** -0.5

SUBLANES = 8
BF16_ROWS = 16
MXU_WIDTH = 256
VMEM_LIMIT_BYTES = 56 * 1024 * 1024

BF16 = jnp.bfloat16
F32 = jnp.float32
NT_DIMS = (((1,), (1,)), ((), ()))


def _rms_scale(x):
    return x * lax.rsqrt(jnp.mean(x * x, axis=-1, keepdims=True) + RMS_EPS)


def _params(*semantics):
    return pltpu.CompilerParams(dimension_semantics=semantics, vmem_limit_bytes=VMEM_LIMIT_BYTES)


def _cast_plumbing(weights, n_blocks, index_map):
    specs, shapes = [], []
    for w in weights:
        rows, cols = w.shape
        assert rows % (n_blocks * BF16_ROWS) == 0, (w.shape, n_blocks)
        specs.append(pl.BlockSpec((rows // n_blocks, cols), index_map))
        shapes.append(jax.ShapeDtypeStruct(w.shape, BF16))
    return specs, list(specs), shapes


def _cast_blocks(src_refs, dst_refs):
    for src, dst in zip(src_refs, dst_refs):
        dst[...] = src[...].astype(BF16)


def _ffn_kernel(x_ref, g_ref, wg_ref, wu_ref, wd_ref, gn_ref, *refs, final, n_cast, emit_weights):
    cast_in, refs = refs[:n_cast], refs[n_cast:]
    n_out = 1 if final else 2
    outs, cast_out, refs = refs[:n_out], refs[n_out:n_out + n_cast], refs[n_out + n_cast:]
    own_out, (h_scr, acc_scr) = (refs[:3], refs[3:]) if emit_weights else ((), refs)
    j = pl.program_id(1)

    @pl.when(j == 0)
    def _():
        h_scr[...] = (_rms_scale(x_ref[...]) * g_ref[...]).astype(BF16)
        acc_scr[...] = jnp.zeros_like(acc_scr)
        _cast_blocks(cast_in, cast_out)

    w_gate, w_up, w_down = (ref[...].astype(BF16) for ref in (wg_ref, wu_ref, wd_ref))
    for dst, w in zip(own_out, (w_gate, w_up, w_down)):
        dst[...] = w
    h = h_scr[...]
    gate = jnp.dot(h, w_gate, preferred_element_type=F32)
    up = jnp.dot(h, w_up, preferred_element_type=F32)
    act = (gate * jax.nn.sigmoid(gate) * up).astype(BF16)
    acc_scr[...] += jnp.dot(act, w_down, preferred_element_type=F32)

    @pl.when(j == pl.num_programs(1) - 1)
    def _():
        s = x_ref[...] + 0.5 * acc_scr[...]
        normed = _rms_scale(s) * gn_ref[...]
        if final:
            outs[0][...] = normed
        else:
            outs[0][...] = s
            outs[1][...] = normed.astype(BF16)


def _ffn(x, g, w_gate, w_up, w_down, g_next, *, final, tm, tf=512, cast=(), emit_weights=False):
    m = x.shape[0]
    n_f = D_FF // tf
    up_block = w_up.shape[1] // tf - n_f
    row = pl.BlockSpec((tm, D_MODEL), lambda i, j: (i, 0))
    vec = pl.BlockSpec((1, D_MODEL), lambda i, j: (0, 0))
    gate_or_up = pl.BlockSpec((D_MODEL, tf), lambda i, j: (0, j))
    down = pl.BlockSpec((tf, D_MODEL), lambda i, j: (j, 0))
    out_shapes = [jax.ShapeDtypeStruct((m, D_MODEL), F32)]
    if not final:
        out_shapes.append(jax.ShapeDtypeStruct((m, D_MODEL), BF16))
    out_specs = [row] * len(out_shapes)
    cast_in, cast_out, cast_shapes = _cast_plumbing(cast, m // tm, lambda i, j: (i, 0))
    if emit_weights:
        assert m == tm
        cast_out = cast_out + [gate_or_up, gate_or_up, down]
        cast_shapes = cast_shapes + [jax.ShapeDtypeStruct(s, BF16) for s in
                                     ((D_MODEL, D_FF), (D_MODEL, D_FF), (D_FF, D_MODEL))]
    return pl.pallas_call(
        functools.partial(_ffn_kernel, final=final, n_cast=len(cast), emit_weights=emit_weights),
        grid=(m // tm, n_f),
        in_specs=[row, vec, gate_or_up, pl.BlockSpec((D_MODEL, tf), lambda i, j: (0, up_block + j)), down, vec]
                 + cast_in,
        out_specs=out_specs + cast_out,
        out_shape=out_shapes + cast_shapes,
        scratch_shapes=[pltpu.VMEM((tm, D_MODEL), BF16), pltpu.VMEM((tm, D_MODEL), F32)],
        compiler_params=_params("parallel", "arbitrary"),
        name="ffn_final" if final else "ffn_mix",
    )(x, g.reshape(1, D_MODEL), w_gate, w_up, w_down, g_next.reshape(1, D_MODEL), *cast)


def _mixer_proj_kernel(h_ref, w_ref, cw_ref, cin_ref, qk_ref, v_ref, y_ref, cout_ref, carry_scr, *, transpose_v):
    i = pl.program_id(0)
    tm = h_ref.shape[0]

    @pl.when(i == 0)
    def _():
        carry_scr[...] = cin_ref[...]

    h = h_ref[...]

    def proj(first_col, width):
        return jnp.dot(h, w_ref[:, first_col:first_col + width], preferred_element_type=F32)

    qk_ref[:, 0:ATT_WIDTH] = (proj(0, ATT_WIDTH) * Q_SCALE).astype(BF16)
    qk_ref[:, ATT_WIDTH:2 * ATT_WIDTH] = proj(ATT_WIDTH, ATT_WIDTH).astype(BF16)
    v_proj = proj(2 * ATT_WIDTH, ATT_WIDTH)
    v_ref[...] = (v_proj.T if transpose_v else v_proj).astype(BF16)

    c0 = 3 * ATT_WIDTH
    b_gate = proj(c0, CONV_WIDTH)
    v = proj(c0 + CONV_WIDTH, CONV_WIDTH) * proj(c0 + 2 * CONV_WIDTH, CONV_WIDTH)
    w0, w1, w2 = cw_ref[0:1, :], cw_ref[1:2, :], cw_ref[2:3, :]

    y = w0 * pltpu.roll(v, 2, axis=0) + w1 * pltpu.roll(v, 1, axis=0) + w2 * v
    y_ref[...] = (b_gate * y).astype(BF16)

    prev = carry_scr[...]
    head = v[0:SUBLANES, :]
    rows = lax.broadcasted_iota(jnp.int32, head.shape, 0)
    v1 = jnp.where(rows >= 1, pltpu.roll(head, 1, axis=0), pltpu.roll(prev, 1, axis=0))
    v2 = jnp.where(rows >= 2, pltpu.roll(head, 2, axis=0), pltpu.roll(prev, 2, axis=0))
    y_head = w0 * v2 + w1 * v1 + w2 * head
    y_ref[0:SUBLANES, :] = (b_gate[0:SUBLANES, :] * y_head).astype(BF16)

    tail = v[tm - SUBLANES:tm, :]
    carry_scr[...] = tail
    cout_ref[...] = tail


def _mixer_proj(h, w_in, conv_w, carry_in, *, tm, transpose_v):
    m = h.shape[0]
    whole = functools.partial(pl.BlockSpec, index_map=lambda i: (0, 0), pipeline_mode=pl.Buffered(1))
    if transpose_v:
        v_spec, v_shape = pl.BlockSpec((ATT_WIDTH, tm), lambda i: (0, i)), (ATT_WIDTH, m)
    else:
        v_spec, v_shape = pl.BlockSpec((tm, ATT_WIDTH), lambda i: (i, 0)), (m, ATT_WIDTH)
    return pl.pallas_call(
        functools.partial(_mixer_proj_kernel, transpose_v=transpose_v),
        grid=(m // tm,),
        in_specs=[pl.BlockSpec((tm, D_MODEL), lambda i: (i, 0)),
                  whole(w_in.shape), whole((CONV_K, CONV_WIDTH)), whole((SUBLANES, CONV_WIDTH))],
        out_specs=(pl.BlockSpec((tm, 2 * ATT_WIDTH), lambda i: (i, 0)),
                   v_spec,
                   pl.BlockSpec((tm, CONV_WIDTH), lambda i: (i, 0)),
                   pl.BlockSpec((SUBLANES, CONV_WIDTH), lambda i: (i, 0))),
        out_shape=(jax.ShapeDtypeStruct((m, 2 * ATT_WIDTH), BF16),
                   jax.ShapeDtypeStruct(v_shape, BF16),
                   jax.ShapeDtypeStruct((m, CONV_WIDTH), BF16),
                   jax.ShapeDtypeStruct((m // tm * SUBLANES, CONV_WIDTH), F32)),
        scratch_shapes=[pltpu.VMEM((SUBLANES, CONV_WIDTH), F32)],
        compiler_params=_params("arbitrary"),
        name="mixer_proj",
    )(h, w_in, conv_w, carry_in)


def _attn_kernel(q_ref, k_ref, vt_ref, km_ref, vtm_ref, lq1_ref, lk1_ref, lq2_ref, lk2_ref, g_ref, *refs,
                 t, tk, cq, tiles_per_iter, n_cast, steps_per_cast):
    cast_in, o_ref, cast_out = refs[:n_cast], refs[n_cast], refs[n_cast + 1:2 * n_cast + 1]
    q2_scr, s_scr, m_scr, l_scr, acc_scr = refs[2 * n_cast + 1:]
    i = pl.program_id(1)
    n_chunks = 2 * t // cq

    @pl.when(i % steps_per_cast == 0)
    def _():
        _cast_blocks(cast_in, cast_out)

    q = q_ref[...]
    lane = lax.broadcasted_iota(jnp.int32, q.shape, 1)
    zero = jnp.zeros_like(q)
    q2_scr[0:t, :] = jnp.where(lane < HEAD_DIM, q, zero)
    q2_scr[t:2 * t, :] = jnp.where(lane >= HEAD_DIM, q, zero)

    def scores(k, c):
        return lax.dot_general(k, q2_scr[c * cq:(c + 1) * cq, :], NT_DIMS, preferred_element_type=F32)

    def keys_of(base, item):
        if item[0] is None:
            return km_ref[...], vtm_ref[...]
        return k_ref[pl.ds(base + item[0], tk), :], vt_ref[:, pl.ds(base + item[0], tk)]

    def chunk_update(c, s_ref, vt, key_offset):
        cols = slice(c * cq, (c + 1) * cq)

        def load_scores():
            s = s_ref[...]
            if key_offset is None:
                return s
            key = key_offset + lax.broadcasted_iota(jnp.int32, s.shape, 0)
            query = (c * cq) % t + lax.broadcasted_iota(jnp.int32, s.shape, 1)
            return jnp.where(key <= query, s, MASKED)

        m_old = m_scr[:, cols]
        m_new = jnp.maximum(m_old, jnp.max(load_scores(), axis=0, keepdims=True))
        alpha = jnp.exp2(m_old - m_new)
        p = jnp.exp2(load_scores() - m_new)
        m_scr[:, cols] = m_new
        l_scr[:, cols] = alpha * l_scr[:, cols] + jnp.sum(p, axis=0, keepdims=True)
        acc_scr[:, cols] = alpha * acc_scr[:, cols] + jnp.dot(vt, p.astype(BF16), preferred_element_type=F32)

    n_slots = s_scr.shape[0]
    ahead = n_slots // 2

    def issue_scores(base, item, slot):
        k = keys_of(base, item)[0]
        s_scr[slot, 0:k.shape[0], :] = scores(k, item[1])

    def run(base, items, next_items):
        assert len(items) % n_slots == 0 or not next_items
        upcoming = items + next_items
        for n, item in enumerate(items):
            if n + ahead < len(upcoming):
                issue_scores(base, upcoming[n + ahead], (n + ahead) % n_slots)
            vt = keys_of(base, item)[1]
            chunk_update(item[1], s_scr.at[n % n_slots, 0:vt.shape[1], :], vt, item[2])

    visible_items = [(d * tk, c, None) for d in range(t // tk) for c in range(n_chunks)]
    diagonal_items = []
    for d in range(t // tk):
        for c in range(n_chunks):
            first_query = (c * cq) % t
            if d * tk <= first_query + cq - 1:
                fully_visible = d * tk + tk - 1 <= first_query
                diagonal_items.append((d * tk, c, None if fully_visible else d * tk))
    diagonal_items += [(None, c, None) for c in range(n_chunks)]

    m_scr[...] = jnp.full_like(m_scr, MASKED)
    l_scr[...] = jnp.zeros_like(l_scr)
    acc_scr[...] = jnp.zeros_like(acc_scr)
    assert [item[:2] for item in diagonal_items[:ahead]] == [item[:2] for item in visible_items[:ahead]]
    for n in range(ahead):
        issue_scores(0, visible_items[n], n)

    def shifted(items, tiles):
        return [(tiles * t + rel, c, key_offset) for rel, c, key_offset in items]

    def body(g, carry):
        group = [item for n in range(tiles_per_iter) for item in shifted(visible_items, n)]
        run(pl.multiple_of(g * tiles_per_iter * t, tiles_per_iter * t), group,
            shifted(visible_items[:ahead], tiles_per_iter))
        return carry

    lax.fori_loop(0, i // tiles_per_iter, body, 0)
    for left in range(1, tiles_per_iter):
        @pl.when(i % tiles_per_iter == left)
        def _():
            group = [item for n in range(left) for item in shifted(visible_items, n)]
            run(pl.multiple_of((i - left) * t, t), group, shifted(visible_items[:ahead], left))

    run(pl.multiple_of(i * t, t), diagonal_items, [])

    lam = (jnp.exp(jnp.sum(lq1_ref[...] * lk1_ref[...], axis=-1, keepdims=True))
           - jnp.exp(jnp.sum(lq2_ref[...] * lk2_ref[...], axis=-1, keepdims=True)) + LAMBDA_INIT)
    o = acc_scr[...] / l_scr[...]
    out = o[:, 0:t] - lam * o[:, t:2 * t]
    normed = out * lax.rsqrt(jnp.mean(out * out, axis=0, keepdims=True) + RMS_EPS)
    o_ref[...] = ((normed * g_ref[...]) * (1.0 - LAMBDA_INIT)).T.astype(BF16)


def _attention(qk, vt, qk_meta, vt_meta, lq1, lk1, lq2, lk2, subln, *, t=2048, tk=512, cq=2 * MXU_WIDTH,
               tiles_per_iter=1, cast=(), steps_per_cast=2):
    m = qk.shape[0]
    lam_vec = pl.BlockSpec((1, HEAD_DIM), lambda h, i: (0, 0))
    casts_per_head = m // t // steps_per_cast
    cast_in, cast_out, cast_shapes = _cast_plumbing(
        cast, N_HEADS * casts_per_head, lambda h, i: (h * casts_per_head + i // steps_per_cast, 0))
    return pl.pallas_call(
        functools.partial(_attn_kernel, t=t, tk=tk, cq=cq, tiles_per_iter=tiles_per_iter, n_cast=len(cast),
                          steps_per_cast=steps_per_cast),
        grid=(N_HEADS, m // t),
        in_specs=[
            pl.BlockSpec((t, V_DIM), lambda h, i: (i, h)),
            pl.BlockSpec((m, V_DIM), lambda h, i: (0, N_HEADS + h)),
            pl.BlockSpec((V_DIM, m), lambda h, i: (h, 0)),
            pl.BlockSpec((N_META, V_DIM), lambda h, i: (0, N_HEADS + h)),
            pl.BlockSpec((V_DIM, N_META), lambda h, i: (h, 0)),
            lam_vec, lam_vec, lam_vec, lam_vec,
            pl.BlockSpec((V_DIM, 1), lambda h, i: (0, 0)),
        ] + cast_in,
        out_specs=[pl.BlockSpec((t, V_DIM), lambda h, i: (i, h))] + cast_out,
        out_shape=[jax.ShapeDtypeStruct((m, ATT_WIDTH), BF16)] + cast_shapes,
        scratch_shapes=[pltpu.VMEM((2 * t, V_DIM), BF16), pltpu.VMEM((4, tk, cq), F32),
                        pltpu.VMEM((1, 2 * t), F32),
                        pltpu.VMEM((1, 2 * t), F32), pltpu.VMEM((V_DIM, 2 * t), F32)],
        compiler_params=_params("parallel", "arbitrary"),
        name="diff_attention",
    )(qk, qk, vt, qk_meta, vt_meta,
      lq1.reshape(1, HEAD_DIM), lk1.reshape(1, HEAD_DIM), lq2.reshape(1, HEAD_DIM), lk2.reshape(1, HEAD_DIM),
      subln.reshape(V_DIM, 1), *cast)


def _merge_kernel(h_ref, ya_ref, yc_ref, s_ref, wg0_ref, wg1_ref, wb0_ref, wb1_ref, wo_ref, o_ref, acc_scr):
    j = pl.program_id(1)

    @pl.when(j == 0)
    def _():
        acc_scr[...] = jnp.zeros_like(acc_scr)

    h = h_ref[...]
    gate_a = jax.nn.sigmoid(jnp.dot(h, wg0_ref[...], preferred_element_type=F32))
    gate_c = jax.nn.sigmoid(jnp.dot(h, wg1_ref[...], preferred_element_type=F32))
    z_a = jnp.dot(ya_ref[...], wb0_ref[...], preferred_element_type=F32)
    z_c = jnp.dot(yc_ref[...], wb1_ref[...], preferred_element_type=F32)
    merged = (gate_a * z_a + gate_c * z_c).astype(BF16)
    acc_scr[...] += jnp.dot(merged, wo_ref[...], preferred_element_type=F32)

    @pl.when(j == pl.num_programs(1) - 1)
    def _():
        o_ref[...] = s_ref[...] + acc_scr[...]


def _merge(h, y_att, y_conv, s, w_gate, w_branch, w_out, *, tm=512, tn=512):
    m = h.shape[0]
    n_d = D_MODEL // tn
    return pl.pallas_call(
        _merge_kernel,
        grid=(m // tm, n_d),
        in_specs=[
            pl.BlockSpec((tm, D_MODEL), lambda i, j: (i, 0)),
            pl.BlockSpec((tm, ATT_WIDTH), lambda i, j: (i, 0)),
            pl.BlockSpec((tm, CONV_WIDTH), lambda i, j: (i, 0)),
            pl.BlockSpec((tm, D_MODEL), lambda i, j: (i, 0)),
            pl.BlockSpec((D_MODEL, tn), lambda i, j: (0, j)),
            pl.BlockSpec((D_MODEL, tn), lambda i, j: (0, n_d + j)),
            pl.BlockSpec((None, ATT_WIDTH, tn), lambda i, j: (0, 0, j)),
            pl.BlockSpec((None, CONV_WIDTH, tn), lambda i, j: (1, 0, j)),
            pl.BlockSpec((tn, D_MODEL), lambda i, j: (j, 0)),
        ],
        out_specs=pl.BlockSpec((tm, D_MODEL), lambda i, j: (i, 0)),
        out_shape=jax.ShapeDtypeStruct((m, D_MODEL), F32),
        scratch_shapes=[pltpu.VMEM((tm, D_MODEL), F32)],
        compiler_params=_params("parallel", "arbitrary"),
        name="gated_merge",
    )(h, y_att, y_conv, s, w_gate, w_gate, w_branch, w_branch, w_out)


def kernel(x, meta_tokens, norm_ffn1, ffn1_w_gu, ffn1_w_down, norm_mix, w_in, conv_w, lambda_q1, lambda_k1,
           lambda_q2, lambda_k2, subln, w_branch, w_gate, w_out, norm_ffn2, ffn2_w_gu, ffn2_w_down, norm_final):
    bsz, seq, _ = x.shape
    assert bsz == 1 and meta_tokens.shape == (N_META, D_MODEL)
    xs = x.reshape(seq, D_MODEL)
    _, h_meta, w_gate1, w_up1, w_down1 = _ffn(
        meta_tokens.astype(x.dtype), norm_ffn1[0], ffn1_w_gu[0], ffn1_w_gu[0], ffn1_w_down[0], norm_mix[0],
        final=False, tm=N_META, emit_weights=True)
    s1, h, w_in_bf = _ffn(xs, norm_ffn1[0], w_gate1, w_up1, w_down1, norm_mix[0], final=False, tm=512,
                          cast=(w_in[0],))

    qk_meta, v_meta, _, meta_tail = _mixer_proj(h_meta, w_in_bf, conv_w[0], jnp.zeros((SUBLANES, CONV_WIDTH), F32),
                                                tm=N_META, transpose_v=False)
    qk, vt, y_conv, _ = _mixer_proj(h, w_in_bf, conv_w[0], meta_tail, tm=512, transpose_v=True)

    later_weights = (w_gate[0], w_branch[0].reshape(N_BRANCH * ATT_WIDTH, D_MODEL), w_out[0],
                     ffn2_w_gu[0], ffn2_w_down[0])
    y_att, w_gate_bf, w_branch_bf, w_out_bf, w_gu2, w_d2 = _attention(
        qk, vt, qk_meta, v_meta.T, lambda_q1[0], lambda_k1[0], lambda_q2[0], lambda_k2[0], subln[0],
        cast=later_weights)
    s2 = _merge(h, y_att, y_conv, s1, w_gate_bf, w_branch_bf.reshape(N_BRANCH, ATT_WIDTH, D_MODEL), w_out_bf)
    out, = _ffn(s2, norm_ffn2[0], w_gu2, w_gu2, w_d2, norm_final, final=True, tm=512)
    return out.reshape(bsz, seq, D_MODEL)
```

```python
import functools
import math

import jax
import jax.numpy as jnp
from jax import lax
from jax.experimental import pallas as pl
from jax.experimental.pallas import tpu as pltpu

D_MODEL = 2048
N_META = 16
HEAD_DIM = 64
V_DIM = 2 * HEAD_DIM
ATT_WIDTH = D_MODEL // 2
N_HEADS = ATT_WIDTH // V_DIM
CONV_WIDTH = D_MODEL // 2
CONV_K = 3
N_BRANCH = 2
D_FF = ((8 * D_MODEL // 3 + 255) // 256) * 256
RMS_EPS = 1e-6
LAMBDA_INIT = 0.8 - 0.6 * math.exp(-0.3 * 0)
MASKED = -1e30
Q_SCALE = math.log2(math.e) * HEAD_DIM ** -0.5

SUBLANES = 8
BF16_ROWS = 16
VT_ROWS = V_DIM + BF16_ROWS
MXU_WIDTH = 256
VMEM_LIMIT_BYTES = 56 * 1024 * 1024

BF16 = jnp.bfloat16
F32 = jnp.float32
NT_DIMS = (((1,), (1,)), ((), ()))


def _rms_scale(x):
    return x * lax.rsqrt(jnp.mean(x * x, axis=-1, keepdims=True) + RMS_EPS)


def _params(*semantics):
    return pltpu.CompilerParams(dimension_semantics=semantics, vmem_limit_bytes=VMEM_LIMIT_BYTES)


def _cast_plumbing(weights, n_blocks, index_map):
    specs, shapes = [], []
    for w in weights:
        rows, cols = w.shape
        assert rows % (n_blocks * BF16_ROWS) == 0, (w.shape, n_blocks)
        specs.append(pl.BlockSpec((rows // n_blocks, cols), index_map))
        shapes.append(jax.ShapeDtypeStruct(w.shape, BF16))
    return specs, list(specs), shapes


def _cast_blocks(src_refs, dst_refs):
    for src, dst in zip(src_refs, dst_refs):
        dst[...] = src[...].astype(BF16)


def _ffn_kernel(x_ref, g_ref, wg_ref, wu_ref, wd_ref, gn_ref, *refs, final, n_cast, emit_weights):
    cast_in, refs = refs[:n_cast], refs[n_cast:]
    n_out = 1 if final else 2
    outs, cast_out, refs = refs[:n_out], refs[n_out:n_out + n_cast], refs[n_out + n_cast:]
    own_out, (h_scr, acc_scr) = (refs[:3], refs[3:]) if emit_weights else ((), refs)
    j = pl.program_id(1)

    @pl.when(j == 0)
    def _():
        h_scr[...] = (_rms_scale(x_ref[...]) * g_ref[...]).astype(BF16)
        acc_scr[...] = jnp.zeros_like(acc_scr)
        _cast_blocks(cast_in, cast_out)

    w_gate, w_up, w_down = (ref[...].astype(BF16) for ref in (wg_ref, wu_ref, wd_ref))
    for dst, w in zip(own_out, (w_gate, w_up, w_down)):
        dst[...] = w
    h = h_scr[...]
    gate = jnp.dot(h, w_gate, preferred_element_type=F32)
    up = jnp.dot(h, w_up, preferred_element_type=F32)
    act = (gate * jax.nn.sigmoid(gate) * up).astype(BF16)
    acc_scr[...] += jnp.dot(act, w_down, preferred_element_type=F32)

    @pl.when(j == pl.num_programs(1) - 1)
    def _():
        s = x_ref[...] + 0.5 * acc_scr[...]
        normed = _rms_scale(s) * gn_ref[...]
        if final:
            outs[0][...] = normed
        else:
            outs[0][...] = s
            outs[1][...] = normed.astype(BF16)


def _ffn(x, g, w_gate, w_up, w_down, g_next, *, final, tm, tf=512, cast=(), emit_weights=False):
    m = x.shape[0]
    n_f = D_FF // tf
    up_block = w_up.shape[1] // tf - n_f
    row = pl.BlockSpec((tm, D_MODEL), lambda i, j: (i, 0))
    vec = pl.BlockSpec((1, D_MODEL), lambda i, j: (0, 0))
    gate_or_up = pl.BlockSpec((D_MODEL, tf), lambda i, j: (0, j))
    down = pl.BlockSpec((tf, D_MODEL), lambda i, j: (j, 0))
    out_shapes = [jax.ShapeDtypeStruct((m, D_MODEL), F32)]
    if not final:
        out_shapes.append(jax.ShapeDtypeStruct((m, D_MODEL), BF16))
    out_specs = [row] * len(out_shapes)
    cast_in, cast_out, cast_shapes = _cast_plumbing(cast, m // tm, lambda i, j: (i, 0))
    if emit_weights:
        assert m == tm
        cast_out = cast_out + [gate_or_up, gate_or_up, down]
        cast_shapes = cast_shapes + [jax.ShapeDtypeStruct(s, BF16) for s in
                                     ((D_MODEL, D_FF), (D_MODEL, D_FF), (D_FF, D_MODEL))]
    return pl.pallas_call(
        functools.partial(_ffn_kernel, final=final, n_cast=len(cast), emit_weights=emit_weights),
        grid=(m // tm, n_f),
        in_specs=[row, vec, gate_or_up, pl.BlockSpec((D_MODEL, tf), lambda i, j: (0, up_block + j)), down, vec]
                 + cast_in,
        out_specs=out_specs + cast_out,
        out_shape=out_shapes + cast_shapes,
        scratch_shapes=[pltpu.VMEM((tm, D_MODEL), BF16), pltpu.VMEM((tm, D_MODEL), F32)],
        compiler_params=_params("parallel", "arbitrary"),
        name="ffn_final" if final else "ffn_mix",
    )(x, g.reshape(1, D_MODEL), w_gate, w_up, w_down, g_next.reshape(1, D_MODEL), *cast)


def _mixer_proj_kernel(h_ref, w_ref, cw_ref, cin_ref, qk_ref, v_ref, y_ref, cout_ref, carry_scr, *, transpose_v):
    i = pl.program_id(0)
    tm = h_ref.shape[0]

    @pl.when(i == 0)
    def _():
        carry_scr[...] = cin_ref[...]

    h = h_ref[...]

    def proj(first_col, width):
        return jnp.dot(h, w_ref[:, first_col:first_col + width], preferred_element_type=F32)

    qk_ref[:, 0:ATT_WIDTH] = (proj(0, ATT_WIDTH) * Q_SCALE).astype(BF16)
    qk_ref[:, ATT_WIDTH:2 * ATT_WIDTH] = proj(ATT_WIDTH, ATT_WIDTH).astype(BF16)
    v_proj = proj(2 * ATT_WIDTH, ATT_WIDTH)
    if transpose_v:
        v_t = v_proj.T
        ones_then_zeros = (lax.broadcasted_iota(jnp.int32, (BF16_ROWS, tm), 0) == 0).astype(BF16)
        for head in range(N_HEADS):
            v_ref[head * VT_ROWS:head * VT_ROWS + V_DIM, :] = v_t[head * V_DIM:(head + 1) * V_DIM, :].astype(BF16)
            v_ref[head * VT_ROWS + V_DIM:(head + 1) * VT_ROWS, :] = ones_then_zeros
    else:
        v_ref[...] = v_proj.astype(BF16)

    c0 = 3 * ATT_WIDTH
    b_gate = proj(c0, CONV_WIDTH)
    v = proj(c0 + CONV_WIDTH, CONV_WIDTH) * proj(c0 + 2 * CONV_WIDTH, CONV_WIDTH)
    w0, w1, w2 = cw_ref[0:1, :], cw_ref[1:2, :], cw_ref[2:3, :]

    y = w0 * pltpu.roll(v, 2, axis=0) + w1 * pltpu.roll(v, 1, axis=0) + w2 * v
    y_ref[...] = (b_gate * y).astype(BF16)

    prev = carry_scr[...]
    head = v[0:SUBLANES, :]
    rows = lax.broadcasted_iota(jnp.int32, head.shape, 0)
    v1 = jnp.where(rows >= 1, pltpu.roll(head, 1, axis=0), pltpu.roll(prev, 1, axis=0))
    v2 = jnp.where(rows >= 2, pltpu.roll(head, 2, axis=0), pltpu.roll(prev, 2, axis=0))
    y_head = w0 * v2 + w1 * v1 + w2 * head
    y_ref[0:SUBLANES, :] = (b_gate[0:SUBLANES, :] * y_head).astype(BF16)

    tail = v[tm - SUBLANES:tm, :]
    carry_scr[...] = tail
    cout_ref[...] = tail


def _mixer_proj(h, w_in, conv_w, carry_in, *, tm, transpose_v):
    m = h.shape[0]
    whole = functools.partial(pl.BlockSpec, index_map=lambda i: (0, 0), pipeline_mode=pl.Buffered(1))
    if transpose_v:
        v_spec, v_shape = pl.BlockSpec((N_HEADS * VT_ROWS, tm), lambda i: (0, i)), (N_HEADS * VT_ROWS, m)
    else:
        v_spec, v_shape = pl.BlockSpec((tm, ATT_WIDTH), lambda i: (i, 0)), (m, ATT_WIDTH)
    return pl.pallas_call(
        functools.partial(_mixer_proj_kernel, transpose_v=transpose_v),
        grid=(m // tm,),
        in_specs=[pl.BlockSpec((tm, D_MODEL), lambda i: (i, 0)),
                  whole(w_in.shape), whole((CONV_K, CONV_WIDTH)), whole((SUBLANES, CONV_WIDTH))],
        out_specs=(pl.BlockSpec((tm, 2 * ATT_WIDTH), lambda i: (i, 0)),
                   v_spec,
                   pl.BlockSpec((tm, CONV_WIDTH), lambda i: (i, 0)),
                   pl.BlockSpec((SUBLANES, CONV_WIDTH), lambda i: (i, 0))),
        out_shape=(jax.ShapeDtypeStruct((m, 2 * ATT_WIDTH), BF16),
                   jax.ShapeDtypeStruct(v_shape, BF16),
                   jax.ShapeDtypeStruct((m, CONV_WIDTH), BF16),
                   jax.ShapeDtypeStruct((m // tm * SUBLANES, CONV_WIDTH), F32)),
        scratch_shapes=[pltpu.VMEM((SUBLANES, CONV_WIDTH), F32)],
        compiler_params=_params("arbitrary"),
        name="mixer_proj",
    )(h, w_in, conv_w, carry_in)


def _attn_kernel(q_ref, k_ref, vt_ref, km_ref, vtm_ref, lq1_ref, lk1_ref, lq2_ref, lk2_ref, g_ref, *refs,
                 t, tk, cq, tiles_per_iter, n_cast, steps_per_cast):
    cast_in, o_ref, cast_out = refs[:n_cast], refs[n_cast], refs[n_cast + 1:2 * n_cast + 1]
    q2_scr, s_scr, m_scr, acc_scr = refs[2 * n_cast + 1:]
    i = pl.program_id(1)
    n_chunks = 2 * t // cq

    @pl.when(i % steps_per_cast == 0)
    def _():
        _cast_blocks(cast_in, cast_out)

    q = q_ref[...]
    lane = lax.broadcasted_iota(jnp.int32, q.shape, 1)
    zero = jnp.zeros_like(q)
    q2_scr[0:t, :] = jnp.where(lane < HEAD_DIM, q, zero)
    q2_scr[t:2 * t, :] = jnp.where(lane >= HEAD_DIM, q, zero)

    def scores(k, c):
        return lax.dot_general(k, q2_scr[c * cq:(c + 1) * cq, :], NT_DIMS, preferred_element_type=F32)

    def keys_of(base, item):
        if item[0] is None:
            return km_ref[...], vtm_ref[...]
        return k_ref[pl.ds(base + item[0], tk), :], vt_ref[:, pl.ds(base + item[0], tk)]

    def chunk_update(c, s_ref, vt, key_offset):
        cols = slice(c * cq, (c + 1) * cq)

        def load_scores():
            s = s_ref[...]
            if key_offset is None:
                return s
            key = key_offset + lax.broadcasted_iota(jnp.int32, s.shape, 0)
            query = (c * cq) % t + lax.broadcasted_iota(jnp.int32, s.shape, 1)
            return jnp.where(key <= query, s, MASKED)

        m_old = m_scr[:, cols]
        m_new = jnp.maximum(m_old, jnp.max(load_scores(), axis=0, keepdims=True))
        alpha = jnp.exp2(m_old - m_new)
        p = jnp.exp2(load_scores() - m_new)
        m_scr[:, cols] = m_new
        acc_scr[:, cols] = alpha * acc_scr[:, cols] + jnp.dot(vt, p.astype(BF16), preferred_element_type=F32)

    n_slots = s_scr.shape[0]
    ahead = n_slots // 2

    def issue_scores(base, item, slot):
        k = keys_of(base, item)[0]
        s_scr[slot, 0:k.shape[0], :] = scores(k, item[1])

    def run(base, items, next_items):
        assert len(items) % n_slots == 0 or not next_items
        upcoming = items + next_items
        for n, item in enumerate(items):
            if n + ahead < len(upcoming):
                issue_scores(base, upcoming[n + ahead], (n + ahead) % n_slots)
            vt = keys_of(base, item)[1]
            chunk_update(item[1], s_scr.at[n % n_slots, 0:vt.shape[1], :], vt, item[2])

    visible_items = [(d * tk, c, None) for d in range(t // tk) for c in range(n_chunks)]
    diagonal_items = []
    for d in range(t // tk):
        for c in range(n_chunks):
            first_query = (c * cq) % t
            if d * tk <= first_query + cq - 1:
                fully_visible = d * tk + tk - 1 <= first_query
                diagonal_items.append((d * tk, c, None if fully_visible else d * tk))
    diagonal_items += [(None, c, None) for c in range(n_chunks)]

    m_scr[...] = jnp.full_like(m_scr, MASKED)
    acc_scr[...] = jnp.zeros_like(acc_scr)
    assert [item[:2] for item in diagonal_items[:ahead]] == [item[:2] for item in visible_items[:ahead]]
    for n in range(ahead):
        issue_scores(0, visible_items[n], n)

    def shifted(items, tiles):
        return [(tiles * t + rel, c, key_offset) for rel, c, key_offset in items]

    def body(g, carry):
        group = [item for n in range(tiles_per_iter) for item in shifted(visible_items, n)]
        run(pl.multiple_of(g * tiles_per_iter * t, tiles_per_iter * t), group,
            shifted(visible_items[:ahead], tiles_per_iter))
        return carry

    lax.fori_loop(0, i // tiles_per_iter, body, 0)
    for left in range(1, tiles_per_iter):
        @pl.when(i % tiles_per_iter == left)
        def _():
            group = [item for n in range(left) for item in shifted(visible_items, n)]
            run(pl.multiple_of((i - left) * t, t), group, shifted(visible_items[:ahead], left))

    run(pl.multiple_of(i * t, t), diagonal_items, [])

    lam = (jnp.exp(jnp.sum(lq1_ref[...] * lk1_ref[...], axis=-1, keepdims=True))
           - jnp.exp(jnp.sum(lq2_ref[...] * lk2_ref[...], axis=-1, keepdims=True)) + LAMBDA_INIT)
    o = acc_scr[0:V_DIM, :] / acc_scr[V_DIM:V_DIM + 1, :]
    out = o[:, 0:t] - lam * o[:, t:2 * t]
    normed = out * lax.rsqrt(jnp.mean(out * out, axis=0, keepdims=True) + RMS_EPS)
    o_ref[...] = ((normed * g_ref[...]) * (1.0 - LAMBDA_INIT)).T.astype(BF16)


def _attention(qk, vt, qk_meta, vt_meta, lq1, lk1, lq2, lk2, subln, *, t=2048, tk=512, cq=2 * MXU_WIDTH,
               tiles_per_iter=1, cast=(), steps_per_cast=2):
    m = qk.shape[0]
    lam_vec = pl.BlockSpec((1, HEAD_DIM), lambda h, i: (0, 0))
    casts_per_head = m // t // steps_per_cast
    cast_in, cast_out, cast_shapes = _cast_plumbing(
        cast, N_HEADS * casts_per_head, lambda h, i: (h * casts_per_head + i // steps_per_cast, 0))
    return pl.pallas_call(
        functools.partial(_attn_kernel, t=t, tk=tk, cq=cq, tiles_per_iter=tiles_per_iter, n_cast=len(cast),
                          steps_per_cast=steps_per_cast),
        grid=(N_HEADS, m // t),
        in_specs=[
            pl.BlockSpec((t, V_DIM), lambda h, i: (i, h)),
            pl.BlockSpec((m, V_DIM), lambda h, i: (0, N_HEADS + h)),
            pl.BlockSpec((VT_ROWS, m), lambda h, i: (h, 0)),
            pl.BlockSpec((N_META, V_DIM), lambda h, i: (0, N_HEADS + h)),
            pl.BlockSpec((VT_ROWS, N_META), lambda h, i: (h, 0)),
            lam_vec, lam_vec, lam_vec, lam_vec,
            pl.BlockSpec((V_DIM, 1), lambda h, i: (0, 0)),
        ] + cast_in,
        out_specs=[pl.BlockSpec((t, V_DIM), lambda h, i: (i, h))] + cast_out,
        out_shape=[jax.ShapeDtypeStruct((m, ATT_WIDTH), BF16)] + cast_shapes,
        scratch_shapes=[pltpu.VMEM((2 * t, V_DIM), BF16), pltpu.VMEM((4, tk, cq), F32),
                        pltpu.VMEM((1, 2 * t), F32), pltpu.VMEM((VT_ROWS, 2 * t), F32)],
        compiler_params=_params("parallel", "arbitrary"),
        name="diff_attention",
    )(qk, qk, vt, qk_meta, vt_meta,
      lq1.reshape(1, HEAD_DIM), lk1.reshape(1, HEAD_DIM), lq2.reshape(1, HEAD_DIM), lk2.reshape(1, HEAD_DIM),
      subln.reshape(V_DIM, 1), *cast)


def _merge_kernel(h_ref, ya_ref, yc_ref, s_ref, wg0_ref, wg1_ref, wb0_ref, wb1_ref, wo_ref, o_ref, acc_scr):
    j = pl.program_id(1)

    @pl.when(j == 0)
    def _():
        acc_scr[...] = jnp.zeros_like(acc_scr)

    h = h_ref[...]
    gate_a = jax.nn.sigmoid(jnp.dot(h, wg0_ref[...], preferred_element_type=F32))
    gate_c = jax.nn.sigmoid(jnp.dot(h, wg1_ref[...], preferred_element_type=F32))
    z_a = jnp.dot(ya_ref[...], wb0_ref[...], preferred_element_type=F32)
    z_c = jnp.dot(yc_ref[...], wb1_ref[...], preferred_element_type=F32)
    merged = (gate_a * z_a + gate_c * z_c).astype(BF16)
    acc_scr[...] += jnp.dot(merged, wo_ref[...], preferred_element_type=F32)

    @pl.when(j == pl.num_programs(1) - 1)
    def _():
        o_ref[...] = s_ref[...] + acc_scr[...]


def _merge(h, y_att, y_conv, s, w_gate, w_branch, w_out, *, tm=512, tn=512):
    m = h.shape[0]
    n_d = D_MODEL // tn
    return pl.pallas_call(
        _merge_kernel,
        grid=(m // tm, n_d),
        in_specs=[
            pl.BlockSpec((tm, D_MODEL), lambda i, j: (i, 0)),
            pl.BlockSpec((tm, ATT_WIDTH), lambda i, j: (i, 0)),
            pl.BlockSpec((tm, CONV_WIDTH), lambda i, j: (i, 0)),
            pl.BlockSpec((tm, D_MODEL), lambda i, j: (i, 0)),
            pl.BlockSpec((D_MODEL, tn), lambda i, j: (0, j)),
            pl.BlockSpec((D_MODEL, tn), lambda i, j: (0, n_d + j)),
            pl.BlockSpec((None, ATT_WIDTH, tn), lambda i, j: (0, 0, j)),
            pl.BlockSpec((None, CONV_WIDTH, tn), lambda i, j: (1, 0, j)),
            pl.BlockSpec((tn, D_MODEL), lambda i, j: (j, 0)),
        ],
        out_specs=pl.BlockSpec((tm, D_MODEL), lambda i, j: (i, 0)),
        out_shape=jax.ShapeDtypeStruct((m, D_MODEL), F32),
        scratch_shapes=[pltpu.VMEM((tm, D_MODEL), F32)],
        compiler_params=_params("parallel", "arbitrary"),
        name="gated_merge",
    )(h, y_att, y_conv, s, w_gate, w_gate, w_branch, w_branch, w_out)


def kernel(x, meta_tokens, norm_ffn1, ffn1_w_gu, ffn1_w_down, norm_mix, w_in, conv_w, lambda_q1, lambda_k1,
           lambda_q2, lambda_k2, subln, w_branch, w_gate, w_out, norm_ffn2, ffn2_w_gu, ffn2_w_down, norm_final):
    bsz, seq, _ = x.shape
    assert bsz == 1 and meta_tokens.shape == (N_META, D_MODEL)
    xs = x.reshape(seq, D_MODEL)
    _, h_meta, w_gate1, w_up1, w_down1 = _ffn(
        meta_tokens.astype(x.dtype), norm_ffn1[0], ffn1_w_gu[0], ffn1_w_gu[0], ffn1_w_down[0], norm_mix[0],
        final=False, tm=N_META, emit_weights=True)
    s1, h, w_in_bf = _ffn(xs, norm_ffn1[0], w_gate1, w_up1, w_down1, norm_mix[0], final=False, tm=512,
                          cast=(w_in[0],))

    qk_meta, v_meta, _, meta_tail = _mixer_proj(h_meta, w_in_bf, conv_w[0], jnp.zeros((SUBLANES, CONV_WIDTH), F32),
                                                tm=N_META, transpose_v=False)
    qk, vt, y_conv, _ = _mixer_proj(h, w_in_bf, conv_w[0], meta_tail, tm=512, transpose_v=True)

    later_weights = (w_gate[0], w_branch[0].reshape(N_BRANCH * ATT_WIDTH, D_MODEL), w_out[0],
                     ffn2_w_gu[0], ffn2_w_down[0])
    vt_meta = jnp.concatenate(
        [v_meta.reshape(N_META, N_HEADS, V_DIM).transpose(1, 2, 0),
         jnp.zeros((N_HEADS, BF16_ROWS, N_META), BF16).at[:, 0, :].set(1)], axis=1).reshape(-1, N_META)
    y_att, w_gate_bf, w_branch_bf, w_out_bf, w_gu2, w_d2 = _attention(
        qk, vt, qk_meta, vt_meta, lambda_q1[0], lambda_k1[0], lambda_q2[0], lambda_k2[0], subln[0],
        cast=later_weights)
    s2 = _merge(h, y_att, y_conv, s1, w_gate_bf, w_branch_bf.reshape(N_BRANCH, ATT_WIDTH, D_MODEL), w_out_bf)
    out, = _ffn(s2, norm_ffn2[0], w_gu2, w_gu2, w_d2, norm_final, final=True, tm=512)
    return out.reshape(bsz, seq, D_MODEL)
```

```python
import functools
import math

import jax
import jax.numpy as jnp
from jax import lax
from jax.experimental import pallas as pl
from jax.experimental.pallas import tpu as pltpu

D_MODEL = 2048
N_META = 16
HEAD_DIM = 64
V_DIM = 2 * HEAD_DIM
ATT_WIDTH = D_MODEL // 2
N_HEADS = ATT_WIDTH // V_DIM
CONV_WIDTH = D_MODEL // 2
CONV_K = 3
N_BRANCH = 2
D_FF = ((8 * D_MODEL // 3 + 255) // 256) * 256
RMS_EPS = 1e-6
LAMBDA_INIT = 0.8 - 0.6 * math.exp(-0.3 * 0)
MASKED = -1e30
Q_SCALE = math.log2(math.e) * HEAD_DIM ** -0.5

SUBLANES = 8
BF16_ROWS = 16
VT_ROWS = V_DIM + BF16_ROWS
MXU_WIDTH = 256
VMEM_LIMIT_BYTES = 56 * 1024 * 1024

BF16 = jnp.bfloat16
F32 = jnp.float32
NT_DIMS = (((1,), (1,)), ((), ()))


def _rms_scale(x):
    return x * lax.rsqrt(jnp.mean(x * x, axis=-1, keepdims=True) + RMS_EPS)


def _params(*semantics):
    return pltpu.CompilerParams(dimension_semantics=semantics, vmem_limit_bytes=VMEM_LIMIT_BYTES)


def _cast_plumbing(weights, n_blocks, index_map):
    specs, shapes = [], []
    for w in weights:
        rows, cols = w.shape
        assert rows % (n_blocks * BF16_ROWS) == 0, (w.shape, n_blocks)
        specs.append(pl.BlockSpec((rows // n_blocks, cols), index_map))
        shapes.append(jax.ShapeDtypeStruct(w.shape, BF16))
    return specs, list(specs), shapes


def _cast_blocks(src_refs, dst_refs):
    for src, dst in zip(src_refs, dst_refs):
        dst[...] = src[...].astype(BF16)


def _ffn_kernel(x_ref, g_ref, wg_ref, wu_ref, wd_ref, gn_ref, *refs, final, n_cast, emit_weights):
    cast_in, refs = refs[:n_cast], refs[n_cast:]
    n_out = 1 if final else 2
    outs, cast_out, refs = refs[:n_out], refs[n_out:n_out + n_cast], refs[n_out + n_cast:]
    own_out, (h_scr, acc_scr) = (refs[:3], refs[3:]) if emit_weights else ((), refs)
    j = pl.program_id(1)

    @pl.when(j == 0)
    def _():
        h_scr[...] = (_rms_scale(x_ref[...]) * g_ref[...]).astype(BF16)
        acc_scr[...] = jnp.zeros_like(acc_scr)
        _cast_blocks(cast_in, cast_out)

    w_gate, w_up, w_down = (ref[...].astype(BF16) for ref in (wg_ref, wu_ref, wd_ref))
    for dst, w in zip(own_out, (w_gate, w_up, w_down)):
        dst[...] = w
    h = h_scr[...]
    gate = jnp.dot(h, w_gate, preferred_element_type=F32)
    up = jnp.dot(h, w_up, preferred_element_type=F32)
    act = (gate * jax.nn.sigmoid(gate) * up).astype(BF16)
    acc_scr[...] += jnp.dot(act, w_down, preferred_element_type=F32)

    @pl.when(j == pl.num_programs(1) - 1)
    def _():
        s = x_ref[...] + 0.5 * acc_scr[...]
        normed = _rms_scale(s) * gn_ref[...]
        if final:
            outs[0][...] = normed
        else:
            outs[0][...] = s
            outs[1][...] = normed.astype(BF16)


def _ffn(x, g, w_gate, w_up, w_down, g_next, *, final, tm, tf=512, cast=(), emit_weights=False):
    m = x.shape[0]
    n_f = D_FF // tf
    up_block = w_up.shape[1] // tf - n_f
    row = pl.BlockSpec((tm, D_MODEL), lambda i, j: (i, 0))
    vec = pl.BlockSpec((1, D_MODEL), lambda i, j: (0, 0))
    gate_or_up = pl.BlockSpec((D_MODEL, tf), lambda i, j: (0, j))
    down = pl.BlockSpec((tf, D_MODEL), lambda i, j: (j, 0))
    out_shapes = [jax.ShapeDtypeStruct((m, D_MODEL), F32)]
    if not final:
        out_shapes.append(jax.ShapeDtypeStruct((m, D_MODEL), BF16))
    out_specs = [row] * len(out_shapes)
    cast_in, cast_out, cast_shapes = _cast_plumbing(cast, m // tm, lambda i, j: (i, 0))
    if emit_weights:
        assert m == tm
        cast_out = cast_out + [gate_or_up, gate_or_up, down]
        cast_shapes = cast_shapes + [jax.ShapeDtypeStruct(s, BF16) for s in
                                     ((D_MODEL, D_FF), (D_MODEL, D_FF), (D_FF, D_MODEL))]
    return pl.pallas_call(
        functools.partial(_ffn_kernel, final=final, n_cast=len(cast), emit_weights=emit_weights),
        grid=(m // tm, n_f),
        in_specs=[row, vec, gate_or_up, pl.BlockSpec((D_MODEL, tf), lambda i, j: (0, up_block + j)), down, vec]
                 + cast_in,
        out_specs=out_specs + cast_out,
        out_shape=out_shapes + cast_shapes,
        scratch_shapes=[pltpu.VMEM((tm, D_MODEL), BF16), pltpu.VMEM((tm, D_MODEL), F32)],
        compiler_params=_params("parallel", "arbitrary"),
        name="ffn_final" if final else "ffn_mix",
    )(x, g.reshape(1, D_MODEL), w_gate, w_up, w_down, g_next.reshape(1, D_MODEL), *cast)


def _mixer_proj_kernel(h_ref, w_ref, cw_ref, cin_ref, qk_ref, v_ref, y_ref, cout_ref, carry_scr, *, transpose_v):
    i = pl.program_id(0)
    tm = h_ref.shape[0]

    @pl.when(i == 0)
    def _():
        carry_scr[...] = cin_ref[...]

    h = h_ref[...]

    def proj(first_col, width):
        return jnp.dot(h, w_ref[:, first_col:first_col + width], preferred_element_type=F32)

    qk_ref[:, 0:ATT_WIDTH] = (proj(0, ATT_WIDTH) * Q_SCALE).astype(BF16)
    qk_ref[:, ATT_WIDTH:2 * ATT_WIDTH] = proj(ATT_WIDTH, ATT_WIDTH).astype(BF16)
    v_proj = proj(2 * ATT_WIDTH, ATT_WIDTH)
    if transpose_v:
        v_t = v_proj.T
        ones_then_zeros = (lax.broadcasted_iota(jnp.int32, (BF16_ROWS, tm), 0) == 0).astype(BF16)
        for head in range(N_HEADS):
            v_ref[head * VT_ROWS:head * VT_ROWS + V_DIM, :] = v_t[head * V_DIM:(head + 1) * V_DIM, :].astype(BF16)
            v_ref[head * VT_ROWS + V_DIM:(head + 1) * VT_ROWS, :] = ones_then_zeros
    else:
        v_ref[...] = v_proj.astype(BF16)

    c0 = 3 * ATT_WIDTH
    b_gate = proj(c0, CONV_WIDTH)
    v = proj(c0 + CONV_WIDTH, CONV_WIDTH) * proj(c0 + 2 * CONV_WIDTH, CONV_WIDTH)
    w0, w1, w2 = cw_ref[0:1, :], cw_ref[1:2, :], cw_ref[2:3, :]

    y = w0 * pltpu.roll(v, 2, axis=0) + w1 * pltpu.roll(v, 1, axis=0) + w2 * v
    y_ref[...] = (b_gate * y).astype(BF16)

    prev = carry_scr[...]
    head = v[0:SUBLANES, :]
    rows = lax.broadcasted_iota(jnp.int32, head.shape, 0)
    v1 = jnp.where(rows >= 1, pltpu.roll(head, 1, axis=0), pltpu.roll(prev, 1, axis=0))
    v2 = jnp.where(rows >= 2, pltpu.roll(head, 2, axis=0), pltpu.roll(prev, 2, axis=0))
    y_head = w0 * v2 + w1 * v1 + w2 * head
    y_ref[0:SUBLANES, :] = (b_gate[0:SUBLANES, :] * y_head).astype(BF16)

    tail = v[tm - SUBLANES:tm, :]
    carry_scr[...] = tail
    cout_ref[...] = tail


def _mixer_proj(h, w_in, conv_w, carry_in, *, tm, transpose_v):
    m = h.shape[0]
    whole = functools.partial(pl.BlockSpec, index_map=lambda i: (0, 0), pipeline_mode=pl.Buffered(1))
    if transpose_v:
        v_spec, v_shape = pl.BlockSpec((N_HEADS * VT_ROWS, tm), lambda i: (0, i)), (N_HEADS * VT_ROWS, m)
    else:
        v_spec, v_shape = pl.BlockSpec((tm, ATT_WIDTH), lambda i: (i, 0)), (m, ATT_WIDTH)
    return pl.pallas_call(
        functools.partial(_mixer_proj_kernel, transpose_v=transpose_v),
        grid=(m // tm,),
        in_specs=[pl.BlockSpec((tm, D_MODEL), lambda i: (i, 0)),
                  whole(w_in.shape), whole((CONV_K, CONV_WIDTH)), whole((SUBLANES, CONV_WIDTH))],
        out_specs=(pl.BlockSpec((tm, 2 * ATT_WIDTH), lambda i: (i, 0)),
                   v_spec,
                   pl.BlockSpec((tm, CONV_WIDTH), lambda i: (i, 0)),
                   pl.BlockSpec((SUBLANES, CONV_WIDTH), lambda i: (i, 0))),
        out_shape=(jax.ShapeDtypeStruct((m, 2 * ATT_WIDTH), BF16),
                   jax.ShapeDtypeStruct(v_shape, BF16),
                   jax.ShapeDtypeStruct((m, CONV_WIDTH), BF16),
                   jax.ShapeDtypeStruct((m // tm * SUBLANES, CONV_WIDTH), F32)),
        scratch_shapes=[pltpu.VMEM((SUBLANES, CONV_WIDTH), F32)],
        compiler_params=_params("arbitrary"),
        name="mixer_proj",
    )(h, w_in, conv_w, carry_in)


def _attn_kernel(q_ref, k_ref, vt_ref, km_ref, vtm_ref, lq1_ref, lk1_ref, lq2_ref, lk2_ref, g_ref, *refs,
                 t, tk, cq, n_cast, steps_per_cast):
    cast_in, o_ref, cast_out = refs[:n_cast], refs[n_cast], refs[n_cast + 1:2 * n_cast + 1]
    q2_scr, s_scr, m_scr, acc_scr = refs[2 * n_cast + 1:]
    i = pl.program_id(1)
    n_chunks = 2 * t // cq

    @pl.when(i % steps_per_cast == 0)
    def _():
        _cast_blocks(cast_in, cast_out)

    q = q_ref[...]
    lane = lax.broadcasted_iota(jnp.int32, q.shape, 1)
    zero = jnp.zeros_like(q)
    q2_scr[0:t, :] = jnp.where(lane < HEAD_DIM, q, zero)
    q2_scr[t:2 * t, :] = jnp.where(lane >= HEAD_DIM, q, zero)

    def scores(k, c):
        return lax.dot_general(k, q2_scr[c * cq:(c + 1) * cq, :], NT_DIMS, preferred_element_type=F32)

    def keys_of(base, item):
        if item[0] is None:
            return km_ref[...], vtm_ref[...]
        return k_ref[pl.ds(base + item[0], tk), :], vt_ref[:, pl.ds(base + item[0], tk)]

    def chunk_update(c, s_ref, vt, key_offset):
        cols = slice(c * cq, (c + 1) * cq)

        def load_scores():
            s = s_ref[...]
            if key_offset is None:
                return s
            key = key_offset + lax.broadcasted_iota(jnp.int32, s.shape, 0)
            query = (c * cq) % t + lax.broadcasted_iota(jnp.int32, s.shape, 1)
            return jnp.where(key <= query, s, MASKED)

        m_old = m_scr[:, cols]
        m_new = jnp.maximum(m_old, jnp.max(load_scores(), axis=0, keepdims=True))
        alpha = jnp.exp2(m_old - m_new)
        p = jnp.exp2(load_scores() - m_new)
        m_scr[:, cols] = m_new
        acc_scr[:, cols] = alpha * acc_scr[:, cols] + jnp.dot(vt, p.astype(BF16), preferred_element_type=F32)

    n_slots = s_scr.shape[0]
    ahead = n_slots // 2

    def issue_scores(base, item, slot):
        k = keys_of(base, item)[0]
        s_scr[slot, 0:k.shape[0], :] = scores(k, item[1])

    def run(base, items, next_items):
        assert len(items) % n_slots == 0 or not next_items
        upcoming = items + next_items
        for n, item in enumerate(items):
            if n + ahead < len(upcoming):
                issue_scores(base, upcoming[n + ahead], (n + ahead) % n_slots)
            vt = keys_of(base, item)[1]
            chunk_update(item[1], s_scr.at[n % n_slots, 0:vt.shape[1], :], vt, item[2])

    visible_items = [(d * tk, c, None) for d in range(t // tk) for c in range(n_chunks)]
    diagonal_items = []
    for d in range(t // tk):
        for c in range(n_chunks):
            first_query = (c * cq) % t
            if d * tk <= first_query + cq - 1:
                fully_visible = d * tk + tk - 1 <= first_query
                assert d * tk <= first_query
                diagonal_items.append((d * tk, c, None if fully_visible else d * tk))
    diagonal_items += [(None, c, None) for c in range(n_chunks)]

    m_scr[...] = jnp.full_like(m_scr, MASKED)
    acc_scr[...] = jnp.zeros_like(acc_scr)
    assert [item[:2] for item in diagonal_items[:ahead]] == [item[:2] for item in visible_items[:ahead]]
    for n in range(ahead):
        issue_scores(0, visible_items[n], n)

    def body(g, carry):
        run(pl.multiple_of(g * t, t), visible_items, [(t + rel, c, None) for rel, c, _ in visible_items[:ahead]])
        return carry

    lax.fori_loop(0, i, body, 0)
    run(pl.multiple_of(i * t, t), diagonal_items, [])

    lam = (jnp.exp(jnp.sum(lq1_ref[...] * lk1_ref[...], axis=-1, keepdims=True))
           - jnp.exp(jnp.sum(lq2_ref[...] * lk2_ref[...], axis=-1, keepdims=True)) + LAMBDA_INIT)
    o = acc_scr[0:V_DIM, :] / acc_scr[V_DIM:V_DIM + 1, :]
    out = o[:, 0:t] - lam * o[:, t:2 * t]
    normed = out * lax.rsqrt(jnp.mean(out * out, axis=0, keepdims=True) + RMS_EPS)
    o_ref[...] = ((normed * g_ref[...]) * (1.0 - LAMBDA_INIT)).T.astype(BF16)


def _attention(qk, vt, qk_meta, vt_meta, lq1, lk1, lq2, lk2, subln, *, t=2048, tk=512, cq=2 * MXU_WIDTH,
               cast=(), steps_per_cast=2):
    m = qk.shape[0]
    lam_vec = pl.BlockSpec((1, HEAD_DIM), lambda h, i: (0, 0))
    casts_per_head = m // t // steps_per_cast
    cast_in, cast_out, cast_shapes = _cast_plumbing(
        cast, N_HEADS * casts_per_head, lambda h, i: (h * casts_per_head + i // steps_per_cast, 0))
    return pl.pallas_call(
        functools.partial(_attn_kernel, t=t, tk=tk, cq=cq, n_cast=len(cast), steps_per_cast=steps_per_cast),
        grid=(N_HEADS, m // t),
        in_specs=[
            pl.BlockSpec((t, V_DIM), lambda h, i: (i, h)),
            pl.BlockSpec((m, V_DIM), lambda h, i: (0, N_HEADS + h)),
            pl.BlockSpec((VT_ROWS, m), lambda h, i: (h, 0)),
            pl.BlockSpec((N_META, V_DIM), lambda h, i: (0, N_HEADS + h)),
            pl.BlockSpec((VT_ROWS, N_META), lambda h, i: (h, 0)),
            lam_vec, lam_vec, lam_vec, lam_vec,
            pl.BlockSpec((V_DIM, 1), lambda h, i: (0, 0)),
        ] + cast_in,
        out_specs=[pl.BlockSpec((t, V_DIM), lambda h, i: (i, h))] + cast_out,
        out_shape=[jax.ShapeDtypeStruct((m, ATT_WIDTH), BF16)] + cast_shapes,
        scratch_shapes=[pltpu.VMEM((2 * t, V_DIM), BF16), pltpu.VMEM((4, tk, cq), F32),
                        pltpu.VMEM((1, 2 * t), F32), pltpu.VMEM((VT_ROWS, 2 * t), F32)],
        compiler_params=_params("parallel", "arbitrary"),
        name="diff_attention",
    )(qk, qk, vt, qk_meta, vt_meta,
      lq1.reshape(1, HEAD_DIM), lk1.reshape(1, HEAD_DIM), lq2.reshape(1, HEAD_DIM), lk2.reshape(1, HEAD_DIM),
      subln.reshape(V_DIM, 1), *cast)


def _merge_kernel(h_ref, ya_ref, yc_ref, s_ref, wg0_ref, wg1_ref, wb0_ref, wb1_ref, wo_ref, o_ref, acc_scr):
    j = pl.program_id(1)

    @pl.when(j == 0)
    def _():
        acc_scr[...] = jnp.zeros_like(acc_scr)

    h = h_ref[...]
    gate_a = jax.nn.sigmoid(jnp.dot(h, wg0_ref[...], preferred_element_type=F32))
    gate_c = jax.nn.sigmoid(jnp.dot(h, wg1_ref[...], preferred_element_type=F32))
    z_a = jnp.dot(ya_ref[...], wb0_ref[...], preferred_element_type=F32)
    z_c = jnp.dot(yc_ref[...], wb1_ref[...], preferred_element_type=F32)
    merged = (gate_a * z_a + gate_c * z_c).astype(BF16)
    acc_scr[...] += jnp.dot(merged, wo_ref[...], preferred_element_type=F32)

    @pl.when(j == pl.num_programs(1) - 1)
    def _():
        o_ref[...] = s_ref[...] + acc_scr[...]


def _merge(h, y_att, y_conv, s, w_gate, w_branch, w_out, *, tm=512, tn=512):
    m = h.shape[0]
    n_d = D_MODEL // tn
    return pl.pallas_call(
        _merge_kernel,
        grid=(m // tm, n_d),
        in_specs=[
            pl.BlockSpec((tm, D_MODEL), lambda i, j: (i, 0)),
            pl.BlockSpec((tm, ATT_WIDTH), lambda i, j: (i, 0)),
            pl.BlockSpec((tm, CONV_WIDTH), lambda i, j: (i, 0)),
            pl.BlockSpec((tm, D_MODEL), lambda i, j: (i, 0)),
            pl.BlockSpec((D_MODEL, tn), lambda i, j: (0, j)),
            pl.BlockSpec((D_MODEL, tn), lambda i, j: (0, n_d + j)),
            pl.BlockSpec((None, ATT_WIDTH, tn), lambda i, j: (0, 0, j)),
            pl.BlockSpec((None, CONV_WIDTH, tn), lambda i, j: (1, 0, j)),
            pl.BlockSpec((tn, D_MODEL), lambda i, j: (j, 0)),
        ],
        out_specs=pl.BlockSpec((tm, D_MODEL), lambda i, j: (i, 0)),
        out_shape=jax.ShapeDtypeStruct((m, D_MODEL), F32),
        scratch_shapes=[pltpu.VMEM((tm, D_MODEL), F32)],
        compiler_params=_params("parallel", "arbitrary"),
        name="gated_merge",
    )(h, y_att, y_conv, s, w_gate, w_gate, w_branch, w_branch, w_out)


def kernel(x, meta_tokens, norm_ffn1, ffn1_w_gu, ffn1_w_down, norm_mix, w_in, conv_w, lambda_q1, lambda_k1,
           lambda_q2, lambda_k2, subln, w_branch, w_gate, w_out, norm_ffn2, ffn2_w_gu, ffn2_w_down, norm_final):
    bsz, seq, _ = x.shape
    assert bsz == 1 and meta_tokens.shape == (N_META, D_MODEL)
    xs = x.reshape(seq, D_MODEL)
    _, h_meta, w_gate1, w_up1, w_down1 = _ffn(
        meta_tokens.astype(x.dtype), norm_ffn1[0], ffn1_w_gu[0], ffn1_w_gu[0], ffn1_w_down[0], norm_mix[0],
        final=False, tm=N_META, emit_weights=True)
    s1, h, w_in_bf = _ffn(xs, norm_ffn1[0], w_gate1, w_up1, w_down1, norm_mix[0], final=False, tm=512,
                          cast=(w_in[0],))

    qk_meta, v_meta, _, meta_tail = _mixer_proj(h_meta, w_in_bf, conv_w[0], jnp.zeros((SUBLANES, CONV_WIDTH), F32),
                                                tm=N_META, transpose_v=False)
    qk, vt, y_conv, _ = _mixer_proj(h, w_in_bf, conv_w[0], meta_tail, tm=512, transpose_v=True)

    later_weights = (w_gate[0], w_branch[0].reshape(N_BRANCH * ATT_WIDTH, D_MODEL), w_out[0],
                     ffn2_w_gu[0], ffn2_w_down[0])
    vt_meta = jnp.concatenate(
        [v_meta.reshape(N_META, N_HEADS, V_DIM).transpose(1, 2, 0),
         jnp.zeros((N_HEADS, BF16_ROWS, N_META), BF16).at[:, 0, :].set(1)], axis=1).reshape(-1, N_META)
    y_att, w_gate_bf, w_branch_bf, w_out_bf, w_gu2, w_d2 = _attention(
        qk, vt, qk_meta, vt_meta, lambda_q1[0], lambda_k1[0], lambda_q2[0], lambda_k2[0], subln[0],
        cast=later_weights)
    s2 = _merge(h, y_att, y_conv, s1, w_gate_bf, w_branch_bf.reshape(N_BRANCH, ATT_WIDTH, D_MODEL), w_out_bf)
    out, = _ffn(s2, norm_ffn2[0], w_gu2, w_gu2, w_d2, norm_final, final=True, tm=512)
    return out.reshape(bsz, seq, D_MODEL)
```

```python
import functools
import math

import jax
import jax.numpy as jnp
from jax import lax
from jax.experimental import pallas as pl
from jax.experimental.pallas import tpu as pltpu

D_MODEL = 2048
N_META = 16
HEAD_DIM = 64
V_DIM = 2 * HEAD_DIM
ATT_WIDTH = D_MODEL // 2
N_HEADS = ATT_WIDTH // V_DIM
CONV_WIDTH = D_MODEL // 2
CONV_K = 3
N_BRANCH = 2
D_FF = ((8 * D_MODEL // 3 + 255) // 256) * 256
RMS_EPS = 1e-6
LAMBDA_INIT = 0.8 - 0.6 * math.exp(-0.3 * 0)
MASKED = -1e30
Q_SCALE = math.log2(math.e) * HEAD_DIM ** -0.5

SUBLANES = 8
BF16_ROWS = 16
VT_ROWS = V_DIM + BF16_ROWS
MXU_WIDTH = 256
VMEM_LIMIT_BYTES = 56 * 1024 * 1024

BF16 = jnp.bfloat16
F32 = jnp.float32
NT_DIMS = (((1,), (1,)), ((), ()))


def _rms_scale(x):
    return x * lax.rsqrt(jnp.mean(x * x, axis=-1, keepdims=True) + RMS_EPS)


def _params(*semantics):
    return pltpu.CompilerParams(dimension_semantics=semantics, vmem_limit_bytes=VMEM_LIMIT_BYTES)


def _cast_plumbing(weights, n_blocks, index_map):
    specs, shapes = [], []
    for w in weights:
        rows, cols = w.shape
        assert rows % (n_blocks * BF16_ROWS) == 0, (w.shape, n_blocks)
        specs.append(pl.BlockSpec((rows // n_blocks, cols), index_map))
        shapes.append(jax.ShapeDtypeStruct(w.shape, BF16))
    return specs, list(specs), shapes


def _cast_blocks(src_refs, dst_refs):
    for src, dst in zip(src_refs, dst_refs):
        dst[...] = src[...].astype(BF16)


def _ffn_kernel(x_ref, g_ref, wg_ref, wu_ref, wd_ref, gn_ref, *refs, final, n_cast, emit_weights):
    cast_in, refs = refs[:n_cast], refs[n_cast:]
    n_out = 1 if final else 2
    outs, cast_out, refs = refs[:n_out], refs[n_out:n_out + n_cast], refs[n_out + n_cast:]
    own_out, (h_scr, acc_scr) = (refs[:3], refs[3:]) if emit_weights else ((), refs)
    j, last = pl.program_id(1), pl.num_programs(1) - 1
    tm = h_scr.shape[0]
    halves = [slice(0, tm // 2), slice(tm // 2, tm)] if tm % (2 * MXU_WIDTH) == 0 else [slice(0, tm)]

    def weights():
        ws = [ref[...].astype(BF16) for ref in (wg_ref, wu_ref, wd_ref)]
        for dst, w in zip(own_out, ws):
            dst[...] = w
        return ws

    def gate_up(h, w_gate, w_up):
        return jnp.dot(h, w_gate, preferred_element_type=F32), jnp.dot(h, w_up, preferred_element_type=F32)

    def down(gate, up, w_down):
        act = (gate * jax.nn.sigmoid(gate) * up).astype(BF16)
        return jnp.dot(act, w_down, preferred_element_type=F32)

    @pl.when(j == 0)
    def _():
        _cast_blocks(cast_in, cast_out)
        w_gate, w_up, w_down = weights()
        gate_ups = []
        for rows in halves:
            h = (_rms_scale(x_ref[rows, :]) * g_ref[...]).astype(BF16)
            h_scr[rows, :] = h
            gate_ups.append(gate_up(h, w_gate, w_up))
        for rows, (gate, up) in zip(halves, gate_ups):
            acc_scr[rows, :] = down(gate, up, w_down)

    @pl.when(jnp.logical_and(j > 0, j < last))
    def _():
        w_gate, w_up, w_down = weights()
        acc_scr[...] += down(*gate_up(h_scr[...], w_gate, w_up), w_down)

    @pl.when(j == last)
    def _():
        w_gate, w_up, w_down = weights()
        gate_ups = [gate_up(h_scr[rows, :], w_gate, w_up) for rows in halves]
        for rows, (gate, up) in zip(halves, gate_ups):
            s = x_ref[rows, :] + 0.5 * (acc_scr[rows, :] + down(gate, up, w_down))
            normed = _rms_scale(s) * gn_ref[...]
            if final:
                outs[0][rows, :] = normed
            else:
                outs[0][rows, :] = s
                outs[1][rows, :] = normed.astype(BF16)


def _ffn(x, g, w_gate, w_up, w_down, g_next, *, final, tm, tf=512, cast=(), emit_weights=False):
    m = x.shape[0]
    n_f = D_FF // tf
    assert n_f >= 2
    up_block = w_up.shape[1] // tf - n_f
    row = pl.BlockSpec((tm, D_MODEL), lambda i, j: (i, 0))
    vec = pl.BlockSpec((1, D_MODEL), lambda i, j: (0, 0))
    gate_or_up = pl.BlockSpec((D_MODEL, tf), lambda i, j: (0, j))
    down = pl.BlockSpec((tf, D_MODEL), lambda i, j: (j, 0))
    out_shapes = [jax.ShapeDtypeStruct((m, D_MODEL), F32)]
    if not final:
        out_shapes.append(jax.ShapeDtypeStruct((m, D_MODEL), BF16))
    out_specs = [row] * len(out_shapes)
    cast_in, cast_out, cast_shapes = _cast_plumbing(cast, m // tm, lambda i, j: (i, 0))
    if emit_weights:
        assert m == tm
        cast_out = cast_out + [gate_or_up, gate_or_up, down]
        cast_shapes = cast_shapes + [jax.ShapeDtypeStruct(s, BF16) for s in
                                     ((D_MODEL, D_FF), (D_MODEL, D_FF), (D_FF, D_MODEL))]
    return pl.pallas_call(
        functools.partial(_ffn_kernel, final=final, n_cast=len(cast), emit_weights=emit_weights),
        grid=(m // tm, n_f),
        in_specs=[row, vec, gate_or_up, pl.BlockSpec((D_MODEL, tf), lambda i, j: (0, up_block + j)), down, vec]
                 + cast_in,
        out_specs=out_specs + cast_out,
        out_shape=out_shapes + cast_shapes,
        scratch_shapes=[pltpu.VMEM((tm, D_MODEL), BF16), pltpu.VMEM((tm, D_MODEL), F32)],
        compiler_params=_params("parallel", "arbitrary"),
        name="ffn_final" if final else "ffn_mix",
    )(x, g.reshape(1, D_MODEL), w_gate, w_up, w_down, g_next.reshape(1, D_MODEL), *cast)


def _mixer_proj_kernel(h_ref, w_ref, cw_ref, cin_ref, qk_ref, v_ref, y_ref, cout_ref, carry_scr, *, transpose_v):
    i = pl.program_id(0)
    tm = h_ref.shape[0]

    @pl.when(i == 0)
    def _():
        carry_scr[...] = cin_ref[...]

    h = h_ref[...]

    def proj(first_col, width):
        return jnp.dot(h, w_ref[:, first_col:first_col + width], preferred_element_type=F32)

    qk_ref[:, 0:ATT_WIDTH] = (proj(0, ATT_WIDTH) * Q_SCALE).astype(BF16)
    qk_ref[:, ATT_WIDTH:2 * ATT_WIDTH] = proj(ATT_WIDTH, ATT_WIDTH).astype(BF16)
    v_proj = proj(2 * ATT_WIDTH, ATT_WIDTH)
    if transpose_v:
        v_t = v_proj.T
        ones_then_zeros = (lax.broadcasted_iota(jnp.int32, (BF16_ROWS, tm), 0) == 0).astype(BF16)
        for head in range(N_HEADS):
            v_ref[head * VT_ROWS:head * VT_ROWS + V_DIM, :] = v_t[head * V_DIM:(head + 1) * V_DIM, :].astype(BF16)
            v_ref[head * VT_ROWS + V_DIM:(head + 1) * VT_ROWS, :] = ones_then_zeros
    else:
        v_ref[...] = v_proj.astype(BF16)

    c0 = 3 * ATT_WIDTH
    b_gate = proj(c0, CONV_WIDTH)
    v = proj(c0 + CONV_WIDTH, CONV_WIDTH) * proj(c0 + 2 * CONV_WIDTH, CONV_WIDTH)
    w0, w1, w2 = cw_ref[0:1, :], cw_ref[1:2, :], cw_ref[2:3, :]

    y = w0 * pltpu.roll(v, 2, axis=0) + w1 * pltpu.roll(v, 1, axis=0) + w2 * v
    y_ref[...] = (b_gate * y).astype(BF16)

    prev = carry_scr[...]
    head = v[0:SUBLANES, :]
    rows = lax.broadcasted_iota(jnp.int32, head.shape, 0)
    v1 = jnp.where(rows >= 1, pltpu.roll(head, 1, axis=0), pltpu.roll(prev, 1, axis=0))
    v2 = jnp.where(rows >= 2, pltpu.roll(head, 2, axis=0), pltpu.roll(prev, 2, axis=0))
    y_head = w0 * v2 + w1 * v1 + w2 * head
    y_ref[0:SUBLANES, :] = (b_gate[0:SUBLANES, :] * y_head).astype(BF16)

    tail = v[tm - SUBLANES:tm, :]
    carry_scr[...] = tail
    cout_ref[...] = tail


def _mixer_proj(h, w_in, conv_w, carry_in, *, tm, transpose_v):
    m = h.shape[0]
    whole = functools.partial(pl.BlockSpec, index_map=lambda i: (0, 0), pipeline_mode=pl.Buffered(1))
    if transpose_v:
        v_spec, v_shape = pl.BlockSpec((N_HEADS * VT_ROWS, tm), lambda i: (0, i)), (N_HEADS * VT_ROWS, m)
    else:
        v_spec, v_shape = pl.BlockSpec((tm, ATT_WIDTH), lambda i: (i, 0)), (m, ATT_WIDTH)
    return pl.pallas_call(
        functools.partial(_mixer_proj_kernel, transpose_v=transpose_v),
        grid=(m // tm,),
        in_specs=[pl.BlockSpec((tm, D_MODEL), lambda i: (i, 0)),
                  whole(w_in.shape), whole((CONV_K, CONV_WIDTH)), whole((SUBLANES, CONV_WIDTH))],
        out_specs=(pl.BlockSpec((tm, 2 * ATT_WIDTH), lambda i: (i, 0)),
                   v_spec,
                   pl.BlockSpec((tm, CONV_WIDTH), lambda i: (i, 0)),
                   pl.BlockSpec((SUBLANES, CONV_WIDTH), lambda i: (i, 0))),
        out_shape=(jax.ShapeDtypeStruct((m, 2 * ATT_WIDTH), BF16),
                   jax.ShapeDtypeStruct(v_shape, BF16),
                   jax.ShapeDtypeStruct((m, CONV_WIDTH), BF16),
                   jax.ShapeDtypeStruct((m // tm * SUBLANES, CONV_WIDTH), F32)),
        scratch_shapes=[pltpu.VMEM((SUBLANES, CONV_WIDTH), F32)],
        compiler_params=_params("arbitrary"),
        name="mixer_proj",
    )(h, w_in, conv_w, carry_in)


def _attn_kernel(q_ref, k_ref, vt_ref, km_ref, vtm_ref, lq1_ref, lk1_ref, lq2_ref, lk2_ref, g_ref, *refs,
                 t, tk, cq, n_cast, steps_per_cast):
    cast_in, o_ref, cast_out = refs[:n_cast], refs[n_cast], refs[n_cast + 1:2 * n_cast + 1]
    q2_scr, s_scr, m_scr, acc_scr = refs[2 * n_cast + 1:]
    i = pl.program_id(1)
    n_chunks = 2 * t // cq

    @pl.when(i % steps_per_cast == 0)
    def _():
        _cast_blocks(cast_in, cast_out)

    q = q_ref[...]
    lane = lax.broadcasted_iota(jnp.int32, q.shape, 1)
    zero = jnp.zeros_like(q)
    q2_scr[0:t, :] = jnp.where(lane < HEAD_DIM, q, zero)
    q2_scr[t:2 * t, :] = jnp.where(lane >= HEAD_DIM, q, zero)

    def scores(k, c):
        return lax.dot_general(k, q2_scr[c * cq:(c + 1) * cq, :], NT_DIMS, preferred_element_type=F32)

    def keys_of(base, item):
        if item[0] is None:
            return km_ref[...], vtm_ref[...]
        return k_ref[pl.ds(base + item[0], tk), :], vt_ref[:, pl.ds(base + item[0], tk)]

    def chunk_update(c, s_ref, vt, key_offset):
        cols = slice(c * cq, (c + 1) * cq)

        def load_scores():
            s = s_ref[...]
            if key_offset is None:
                return s
            key = key_offset + lax.broadcasted_iota(jnp.int32, s.shape, 0)
            query = (c * cq) % t + lax.broadcasted_iota(jnp.int32, s.shape, 1)
            return jnp.where(key <= query, s, MASKED)

        m_old = m_scr[:, cols]
        m_new = jnp.maximum(m_old, jnp.max(load_scores(), axis=0, keepdims=True))
        alpha = jnp.exp2(m_old - m_new)
        p = jnp.exp2(load_scores() - m_new)
        m_scr[:, cols] = m_new
        acc_scr[:, cols] = alpha * acc_scr[:, cols] + jnp.dot(vt, p.astype(BF16), preferred_element_type=F32)

    n_slots = s_scr.shape[0]
    ahead = n_slots // 2

    def issue_scores(base, item, slot):
        k = keys_of(base, item)[0]
        s_scr[slot, 0:k.shape[0], :] = scores(k, item[1])

    def run(base, items, next_items):
        assert len(items) % n_slots == 0 or not next_items
        upcoming = items + next_items
        for n, item in enumerate(items):
            if n + ahead < len(upcoming):
                issue_scores(base, upcoming[n + ahead], (n + ahead) % n_slots)
            vt = keys_of(base, item)[1]
            chunk_update(item[1], s_scr.at[n % n_slots, 0:vt.shape[1], :], vt, item[2])

    visible_items = [(d * tk, c, None) for d in range(t // tk) for c in range(n_chunks)]
    diagonal_items = []
    for d in range(t // tk):
        for c in range(n_chunks):
            first_query = (c * cq) % t
            if d * tk <= first_query + cq - 1:
                fully_visible = d * tk + tk - 1 <= first_query
                assert d * tk <= first_query
                diagonal_items.append((d * tk, c, None if fully_visible else d * tk))
    diagonal_items += [(None, c, None) for c in range(n_chunks)]

    m_scr[...] = jnp.full_like(m_scr, MASKED)
    acc_scr[...] = jnp.zeros_like(acc_scr)
    assert [item[:2] for item in diagonal_items[:ahead]] == [item[:2] for item in visible_items[:ahead]]
    for n in range(ahead):
        issue_scores(0, visible_items[n], n)

    def body(g, carry):
        run(pl.multiple_of(g * t, t), visible_items, [(t + rel, c, None) for rel, c, _ in visible_items[:ahead]])
        return carry

    lax.fori_loop(0, i, body, 0)
    run(pl.multiple_of(i * t, t), diagonal_items, [])

    lam = (jnp.exp(jnp.sum(lq1_ref[...] * lk1_ref[...], axis=-1, keepdims=True))
           - jnp.exp(jnp.sum(lq2_ref[...] * lk2_ref[...], axis=-1, keepdims=True)) + LAMBDA_INIT)
    o = acc_scr[0:V_DIM, :] / acc_scr[V_DIM:V_DIM + 1, :]
    out = o[:, 0:t] - lam * o[:, t:2 * t]
    normed = out * lax.rsqrt(jnp.mean(out * out, axis=0, keepdims=True) + RMS_EPS)
    o_ref[...] = ((normed * g_ref[...]) * (1.0 - LAMBDA_INIT)).T.astype(BF16)


def _attention(qk, vt, qk_meta, vt_meta, lq1, lk1, lq2, lk2, subln, *, t=2048, tk=512, cq=2 * MXU_WIDTH,
               cast=(), steps_per_cast=2):
    m = qk.shape[0]
    lam_vec = pl.BlockSpec((1, HEAD_DIM), lambda h, i: (0, 0))
    casts_per_head = m // t // steps_per_cast
    cast_in, cast_out, cast_shapes = _cast_plumbing(
        cast, N_HEADS * casts_per_head, lambda h, i: (h * casts_per_head + i // steps_per_cast, 0))
    return pl.pallas_call(
        functools.partial(_attn_kernel, t=t, tk=tk, cq=cq, n_cast=len(cast), steps_per_cast=steps_per_cast),
        grid=(N_HEADS, m // t),
        in_specs=[
            pl.BlockSpec((t, V_DIM), lambda h, i: (i, h)),
            pl.BlockSpec((m, V_DIM), lambda h, i: (0, N_HEADS + h)),
            pl.BlockSpec((VT_ROWS, m), lambda h, i: (h, 0)),
            pl.BlockSpec((N_META, V_DIM), lambda h, i: (0, N_HEADS + h)),
            pl.BlockSpec((VT_ROWS, N_META), lambda h, i: (h, 0)),
            lam_vec, lam_vec, lam_vec, lam_vec,
            pl.BlockSpec((V_DIM, 1), lambda h, i: (0, 0)),
        ] + cast_in,
        out_specs=[pl.BlockSpec((t, V_DIM), lambda h, i: (i, h))] + cast_out,
        out_shape=[jax.ShapeDtypeStruct((m, ATT_WIDTH), BF16)] + cast_shapes,
        scratch_shapes=[pltpu.VMEM((2 * t, V_DIM), BF16), pltpu.VMEM((4, tk, cq), F32),
                        pltpu.VMEM((1, 2 * t), F32), pltpu.VMEM((VT_ROWS, 2 * t), F32)],
        compiler_params=_params("parallel", "arbitrary"),
        name="diff_attention",
    )(qk, qk, vt, qk_meta, vt_meta,
      lq1.reshape(1, HEAD_DIM), lk1.reshape(1, HEAD_DIM), lq2.reshape(1, HEAD_DIM), lk2.reshape(1, HEAD_DIM),
      subln.reshape(V_DIM, 1), *cast)


def _merge_kernel(h_ref, ya_ref, yc_ref, s_ref, wg0_ref, wg1_ref, wb0_ref, wb1_ref, wo_ref, o_ref, acc_scr):
    j = pl.program_id(1)

    @pl.when(j == 0)
    def _():
        acc_scr[...] = jnp.zeros_like(acc_scr)

    h = h_ref[...]
    gate_a = jax.nn.sigmoid(jnp.dot(h, wg0_ref[...], preferred_element_type=F32))
    gate_c = jax.nn.sigmoid(jnp.dot(h, wg1_ref[...], preferred_element_type=F32))
    z_a = jnp.dot(ya_ref[...], wb0_ref[...], preferred_element_type=F32)
    z_c = jnp.dot(yc_ref[...], wb1_ref[...], preferred_element_type=F32)
    merged = (gate_a * z_a + gate_c * z_c).astype(BF16)
    acc_scr[...] += jnp.dot(merged, wo_ref[...], preferred_element_type=F32)

    @pl.when(j == pl.num_programs(1) - 1)
    def _():
        o_ref[...] = s_ref[...] + acc_scr[...]


def _merge(h, y_att, y_conv, s, w_gate, w_branch, w_out, *, tm=512, tn=512):
    m = h.shape[0]
    n_d = D_MODEL // tn
    return pl.pallas_call(
        _merge_kernel,
        grid=(m // tm, n_d),
        in_specs=[
            pl.BlockSpec((tm, D_MODEL), lambda i, j: (i, 0)),
            pl.BlockSpec((tm, ATT_WIDTH), lambda i, j: (i, 0)),
            pl.BlockSpec((tm, CONV_WIDTH), lambda i, j: (i, 0)),
            pl.BlockSpec((tm, D_MODEL), lambda i, j: (i, 0)),
            pl.BlockSpec((D_MODEL, tn), lambda i, j: (0, j)),
            pl.BlockSpec((D_MODEL, tn), lambda i, j: (0, n_d + j)),
            pl.BlockSpec((None, ATT_WIDTH, tn), lambda i, j: (0, 0, j)),
            pl.BlockSpec((None, CONV_WIDTH, tn), lambda i, j: (1, 0, j)),
            pl.BlockSpec((tn, D_MODEL), lambda i, j: (j, 0)),
        ],
        out_specs=pl.BlockSpec((tm, D_MODEL), lambda i, j: (i, 0)),
        out_shape=jax.ShapeDtypeStruct((m, D_MODEL), F32),
        scratch_shapes=[pltpu.VMEM((tm, D_MODEL), F32)],
        compiler_params=_params("parallel", "arbitrary"),
        name="gated_merge",
    )(h, y_att, y_conv, s, w_gate, w_gate, w_branch, w_branch, w_out)


def kernel(x, meta_tokens, norm_ffn1, ffn1_w_gu, ffn1_w_down, norm_mix, w_in, conv_w, lambda_q1, lambda_k1,
           lambda_q2, lambda_k2, subln, w_branch, w_gate, w_out, norm_ffn2, ffn2_w_gu, ffn2_w_down, norm_final):
    bsz, seq, _ = x.shape
    assert bsz == 1 and meta_tokens.shape == (N_META, D_MODEL)
    xs = x.reshape(seq, D_MODEL)
    _, h_meta, w_gate1, w_up1, w_down1 = _ffn(
        meta_tokens.astype(x.dtype), norm_ffn1[0], ffn1_w_gu[0], ffn1_w_gu[0], ffn1_w_down[0], norm_mix[0],
        final=False, tm=N_META, emit_weights=True)
    s1, h, w_in_bf = _ffn(xs, norm_ffn1[0], w_gate1, w_up1, w_down1, norm_mix[0], final=False, tm=512,
                          cast=(w_in[0],))

    qk_meta, v_meta, _, meta_tail = _mixer_proj(h_meta, w_in_bf, conv_w[0], jnp.zeros((SUBLANES, CONV_WIDTH), F32),
                                                tm=N_META, transpose_v=False)
    qk, vt, y_conv, _ = _mixer_proj(h, w_in_bf, conv_w[0], meta_tail, tm=512, transpose_v=True)

    later_weights = (w_gate[0], w_branch[0].reshape(N_BRANCH * ATT_WIDTH, D_MODEL), w_out[0],
                     ffn2_w_gu[0], ffn2_w_down[0])
    vt_meta = jnp.concatenate(
        [v_meta.reshape(N_META, N_HEADS, V_DIM).transpose(1, 2, 0),
         jnp.zeros((N_HEADS, BF16_ROWS, N_META), BF16).at[:, 0, :].set(1)], axis=1).reshape(-1, N_META)
    y_att, w_gate_bf, w_branch_bf, w_out_bf, w_gu2, w_d2 = _attention(
        qk, vt, qk_meta, vt_meta, lambda_q1[0], lambda_k1[0], lambda_q2[0], lambda_k2[0], subln[0],
        cast=later_weights)
    s2 = _merge(h, y_att, y_conv, s1, w_gate_bf, w_branch_bf.reshape(N_BRANCH, ATT_WIDTH, D_MODEL), w_out_bf)
    out, = _ffn(s2, norm_ffn2[0], w_gu2, w_gu2, w_d2, norm_final, final=True, tm=512)
    return out.reshape(bsz, seq, D_MODEL)
```

```python
import functools
import math

import jax
import jax.numpy as jnp
from jax import lax
from jax.experimental import pallas as pl
from jax.experimental.pallas import tpu as pltpu

D_MODEL = 2048
N_META = 16
HEAD_DIM = 64
V_DIM = 2 * HEAD_DIM
ATT_WIDTH = D_MODEL // 2
N_HEADS = ATT_WIDTH // V_DIM
CONV_WIDTH = D_MODEL // 2
CONV_K = 3
N_BRANCH = 2
D_FF = ((8 * D_MODEL // 3 + 255) // 256) * 256
RMS_EPS = 1e-6
LAMBDA_INIT = 0.8 - 0.6 * math.exp(-0.3 * 0)
MASKED = -1e30
Q_SCALE = math.log2(math.e) * HEAD_DIM ** -0.5

SUBLANES = 8
BF16_ROWS = 16
VT_ROWS = V_DIM + BF16_ROWS
MXU_WIDTH = 256
VMEM_LIMIT_BYTES = 56 * 1024 * 1024

BF16 = jnp.bfloat16
F32 = jnp.float32
NT_DIMS = (((1,), (1,)), ((), ()))


def _rms_scale(x):
    return x * lax.rsqrt(jnp.mean(x * x, axis=-1, keepdims=True) + RMS_EPS)


def _params(*semantics):
    return pltpu.CompilerParams(dimension_semantics=semantics, vmem_limit_bytes=VMEM_LIMIT_BYTES)


def _cast_plumbing(weights, n_blocks, index_map):
    specs, shapes = [], []
    for w in weights:
        rows, cols = w.shape
        assert rows % (n_blocks * BF16_ROWS) == 0, (w.shape, n_blocks)
        specs.append(pl.BlockSpec((rows // n_blocks, cols), index_map))
        shapes.append(jax.ShapeDtypeStruct(w.shape, BF16))
    return specs, list(specs), shapes


def _cast_blocks(src_refs, dst_refs):
    for src, dst in zip(src_refs, dst_refs):
        dst[...] = src[...].astype(BF16)


def _ffn_kernel(x_ref, g_ref, wg_ref, wu_ref, wd_ref, gn_ref, *refs, final, n_cast, emit_weights):
    cast_in, refs = refs[:n_cast], refs[n_cast:]
    n_out = 1 if final else 2
    outs, cast_out, refs = refs[:n_out], refs[n_out:n_out + n_cast], refs[n_out + n_cast:]
    own_out, (h_scr, acc_scr) = (refs[:3], refs[3:]) if emit_weights else ((), refs)
    j, last = pl.program_id(1), pl.num_programs(1) - 1
    tm = h_scr.shape[0]
    halves = [slice(0, tm // 2), slice(tm // 2, tm)] if tm % (2 * MXU_WIDTH) == 0 else [slice(0, tm)]

    def weights():
        ws = [ref[...].astype(BF16) for ref in (wg_ref, wu_ref, wd_ref)]
        for dst, w in zip(own_out, ws):
            dst[...] = w
        return ws

    def gate_up(h, w_gate, w_up):
        return jnp.dot(h, w_gate, preferred_element_type=F32), jnp.dot(h, w_up, preferred_element_type=F32)

    def down(gate, up, w_down):
        act = (gate * jax.nn.sigmoid(gate) * up).astype(BF16)
        return jnp.dot(act, w_down, preferred_element_type=F32)

    @pl.when(j == 0)
    def _():
        _cast_blocks(cast_in, cast_out)
        w_gate, w_up, w_down = weights()
        gate_ups = []
        for rows in halves:
            h = (_rms_scale(x_ref[rows, :]) * g_ref[...]).astype(BF16)
            h_scr[rows, :] = h
            gate_ups.append(gate_up(h, w_gate, w_up))
        for rows, (gate, up) in zip(halves, gate_ups):
            acc_scr[rows, :] = down(gate, up, w_down)

    @pl.when(jnp.logical_and(j > 0, j < last))
    def _():
        w_gate, w_up, w_down = weights()
        acc_scr[...] += down(*gate_up(h_scr[...], w_gate, w_up), w_down)

    @pl.when(j == last)
    def _():
        w_gate, w_up, w_down = weights()
        gate_ups = [gate_up(h_scr[rows, :], w_gate, w_up) for rows in halves]
        for rows, (gate, up) in zip(halves, gate_ups):
            s = x_ref[rows, :] + 0.5 * (acc_scr[rows, :] + down(gate, up, w_down))
            normed = _rms_scale(s) * gn_ref[...]
            if final:
                outs[0][rows, :] = normed
            else:
                outs[0][rows, :] = s
                outs[1][rows, :] = normed.astype(BF16)


def _ffn(x, g, w_gate, w_up, w_down, g_next, *, final, tm, tf=512, cast=(), emit_weights=False):
    m = x.shape[0]
    n_f = D_FF // tf
    assert n_f >= 2
    row = pl.BlockSpec((tm, D_MODEL), lambda i, j: (i, 0))
    vec = pl.BlockSpec((1, D_MODEL), lambda i, j: (0, 0))
    chunk_major = pl.BlockSpec((None, D_MODEL, tf), lambda i, j: (j, 0, 0))
    down = pl.BlockSpec((tf, D_MODEL), lambda i, j: (j, 0))

    def column_chunk(w, first_block):
        if w.ndim == 3:
            return chunk_major
        return pl.BlockSpec((D_MODEL, tf), lambda i, j: (0, first_block + j))

    out_shapes = [jax.ShapeDtypeStruct((m, D_MODEL), F32)]
    if not final:
        out_shapes.append(jax.ShapeDtypeStruct((m, D_MODEL), BF16))
    out_specs = [row] * len(out_shapes)
    cast_in, cast_out, cast_shapes = _cast_plumbing(cast, m // tm, lambda i, j: (i, 0))
    if emit_weights:
        assert m == tm
        cast_out = cast_out + [chunk_major, chunk_major, down]
        cast_shapes = cast_shapes + [jax.ShapeDtypeStruct(s, BF16) for s in
                                     ((n_f, D_MODEL, tf), (n_f, D_MODEL, tf), (D_FF, D_MODEL))]
    return pl.pallas_call(
        functools.partial(_ffn_kernel, final=final, n_cast=len(cast), emit_weights=emit_weights),
        grid=(m // tm, n_f),
        in_specs=[row, vec, column_chunk(w_gate, 0), column_chunk(w_up, w_up.shape[-1] // tf - n_f), down, vec]
                 + cast_in,
        out_specs=out_specs + cast_out,
        out_shape=out_shapes + cast_shapes,
        scratch_shapes=[pltpu.VMEM((tm, D_MODEL), BF16), pltpu.VMEM((tm, D_MODEL), F32)],
        compiler_params=_params("parallel", "arbitrary"),
        name="ffn_final" if final else "ffn_mix",
    )(x, g.reshape(1, D_MODEL), w_gate, w_up, w_down, g_next.reshape(1, D_MODEL), *cast)


def _mixer_proj_kernel(h_ref, w_ref, cw_ref, cin_ref, qk_ref, v_ref, y_ref, cout_ref, carry_scr, *, transpose_v):
    i = pl.program_id(0)
    tm = h_ref.shape[0]

    @pl.when(i == 0)
    def _():
        carry_scr[...] = cin_ref[...]

    h = h_ref[...]

    def proj(first_col, width):
        return jnp.dot(h, w_ref[:, first_col:first_col + width], preferred_element_type=F32)

    qk_ref[:, 0:ATT_WIDTH] = (proj(0, ATT_WIDTH) * Q_SCALE).astype(BF16)
    qk_ref[:, ATT_WIDTH:2 * ATT_WIDTH] = proj(ATT_WIDTH, ATT_WIDTH).astype(BF16)
    v_proj = proj(2 * ATT_WIDTH, ATT_WIDTH)
    if transpose_v:
        v_t = v_proj.T
        ones_then_zeros = (lax.broadcasted_iota(jnp.int32, (BF16_ROWS, tm), 0) == 0).astype(BF16)
        for head in range(N_HEADS):
            v_ref[head * VT_ROWS:head * VT_ROWS + V_DIM, :] = v_t[head * V_DIM:(head + 1) * V_DIM, :].astype(BF16)
            v_ref[head * VT_ROWS + V_DIM:(head + 1) * VT_ROWS, :] = ones_then_zeros
    else:
        v_ref[...] = v_proj.astype(BF16)

    c0 = 3 * ATT_WIDTH
    b_gate = proj(c0, CONV_WIDTH)
    v = proj(c0 + CONV_WIDTH, CONV_WIDTH) * proj(c0 + 2 * CONV_WIDTH, CONV_WIDTH)
    w0, w1, w2 = cw_ref[0:1, :], cw_ref[1:2, :], cw_ref[2:3, :]

    y = w0 * pltpu.roll(v, 2, axis=0) + w1 * pltpu.roll(v, 1, axis=0) + w2 * v
    y_ref[...] = (b_gate * y).astype(BF16)

    prev = carry_scr[...]
    head = v[0:SUBLANES, :]
    rows = lax.broadcasted_iota(jnp.int32, head.shape, 0)
    v1 = jnp.where(rows >= 1, pltpu.roll(head, 1, axis=0), pltpu.roll(prev, 1, axis=0))
    v2 = jnp.where(rows >= 2, pltpu.roll(head, 2, axis=0), pltpu.roll(prev, 2, axis=0))
    y_head = w0 * v2 + w1 * v1 + w2 * head
    y_ref[0:SUBLANES, :] = (b_gate[0:SUBLANES, :] * y_head).astype(BF16)

    tail = v[tm - SUBLANES:tm, :]
    carry_scr[...] = tail
    cout_ref[...] = tail


def _mixer_proj(h, w_in, conv_w, carry_in, *, tm, transpose_v):
    m = h.shape[0]
    whole = functools.partial(pl.BlockSpec, index_map=lambda i: (0, 0), pipeline_mode=pl.Buffered(1))
    if transpose_v:
        v_spec, v_shape = pl.BlockSpec((N_HEADS * VT_ROWS, tm), lambda i: (0, i)), (N_HEADS * VT_ROWS, m)
    else:
        v_spec, v_shape = pl.BlockSpec((tm, ATT_WIDTH), lambda i: (i, 0)), (m, ATT_WIDTH)
    return pl.pallas_call(
        functools.partial(_mixer_proj_kernel, transpose_v=transpose_v),
        grid=(m // tm,),
        in_specs=[pl.BlockSpec((tm, D_MODEL), lambda i: (i, 0)),
                  whole(w_in.shape), whole((CONV_K, CONV_WIDTH)), whole((SUBLANES, CONV_WIDTH))],
        out_specs=(pl.BlockSpec((tm, 2 * ATT_WIDTH), lambda i: (i, 0)),
                   v_spec,
                   pl.BlockSpec((tm, CONV_WIDTH), lambda i: (i, 0)),
                   pl.BlockSpec((SUBLANES, CONV_WIDTH), lambda i: (i, 0))),
        out_shape=(jax.ShapeDtypeStruct((m, 2 * ATT_WIDTH), BF16),
                   jax.ShapeDtypeStruct(v_shape, BF16),
                   jax.ShapeDtypeStruct((m, CONV_WIDTH), BF16),
                   jax.ShapeDtypeStruct((m // tm * SUBLANES, CONV_WIDTH), F32)),
        scratch_shapes=[pltpu.VMEM((SUBLANES, CONV_WIDTH), F32)],
        compiler_params=_params("arbitrary"),
        name="mixer_proj",
    )(h, w_in, conv_w, carry_in)


def _attn_kernel(q_ref, k_ref, vt_ref, km_ref, vtm_ref, lq1_ref, lk1_ref, lq2_ref, lk2_ref, g_ref, *refs,
                 t, tk, cq, n_cast, steps_per_cast):
    cast_in, o_ref, cast_out = refs[:n_cast], refs[n_cast], refs[n_cast + 1:2 * n_cast + 1]
    q2_scr, s_scr, m_scr, acc_scr = refs[2 * n_cast + 1:]
    i = pl.program_id(1)
    n_chunks = 2 * t // cq

    @pl.when(i % steps_per_cast == 0)
    def _():
        _cast_blocks(cast_in, cast_out)

    q = q_ref[...]
    lane = lax.broadcasted_iota(jnp.int32, q.shape, 1)
    zero = jnp.zeros_like(q)
    q2_scr[0:t, :] = jnp.where(lane < HEAD_DIM, q, zero)
    q2_scr[t:2 * t, :] = jnp.where(lane >= HEAD_DIM, q, zero)

    def scores(k, c):
        return lax.dot_general(k, q2_scr[c * cq:(c + 1) * cq, :], NT_DIMS, preferred_element_type=F32)

    def keys_of(base, item):
        if item[0] is None:
            return km_ref[...], vtm_ref[...]
        return k_ref[pl.ds(base + item[0], tk), :], vt_ref[:, pl.ds(base + item[0], tk)]

    def chunk_update(c, s_ref, vt, key_offset):
        cols = slice(c * cq, (c + 1) * cq)

        def load_scores():
            s = s_ref[...]
            if key_offset is None:
                return s
            key = key_offset + lax.broadcasted_iota(jnp.int32, s.shape, 0)
            query = (c * cq) % t + lax.broadcasted_iota(jnp.int32, s.shape, 1)
            return jnp.where(key <= query, s, MASKED)

        m_old = m_scr[:, cols]
        m_new = jnp.maximum(m_old, jnp.max(load_scores(), axis=0, keepdims=True))
        alpha = jnp.exp2(m_old - m_new)
        p = jnp.exp2(load_scores() - m_new)
        m_scr[:, cols] = m_new
        acc_scr[:, cols] = alpha * acc_scr[:, cols] + jnp.dot(vt, p.astype(BF16), preferred_element_type=F32)

    n_slots = s_scr.shape[0]
    ahead = n_slots // 2

    def issue_scores(base, item, slot):
        k = keys_of(base, item)[0]
        s_scr[slot, 0:k.shape[0], :] = scores(k, item[1])

    def run(base, items, next_items):
        assert len(items) % n_slots == 0 or not next_items
        upcoming = items + next_items
        for n, item in enumerate(items):
            if n + ahead < len(upcoming):
                issue_scores(base, upcoming[n + ahead], (n + ahead) % n_slots)
            vt = keys_of(base, item)[1]
            chunk_update(item[1], s_scr.at[n % n_slots, 0:vt.shape[1], :], vt, item[2])

    visible_items = [(d * tk, c, None) for d in range(t // tk) for c in range(n_chunks)]
    diagonal_items = []
    for d in range(t // tk):
        for c in range(n_chunks):
            first_query = (c * cq) % t
            if d * tk <= first_query + cq - 1:
                fully_visible = d * tk + tk - 1 <= first_query
                assert d * tk <= first_query
                diagonal_items.append((d * tk, c, None if fully_visible else d * tk))
    diagonal_items += [(None, c, None) for c in range(n_chunks)]

    m_scr[...] = jnp.full_like(m_scr, MASKED)
    acc_scr[...] = jnp.zeros_like(acc_scr)
    assert [item[:2] for item in diagonal_items[:ahead]] == [item[:2] for item in visible_items[:ahead]]
    for n in range(ahead):
        issue_scores(0, visible_items[n], n)

    def body(g, carry):
        run(pl.multiple_of(g * t, t), visible_items, [(t + rel, c, None) for rel, c, _ in visible_items[:ahead]])
        return carry

    lax.fori_loop(0, i, body, 0)
    run(pl.multiple_of(i * t, t), diagonal_items, [])

    lam = (jnp.exp(jnp.sum(lq1_ref[...] * lk1_ref[...], axis=-1, keepdims=True))
           - jnp.exp(jnp.sum(lq2_ref[...] * lk2_ref[...], axis=-1, keepdims=True)) + LAMBDA_INIT)
    o = acc_scr[0:V_DIM, :] / acc_scr[V_DIM:V_DIM + 1, :]
    out = o[:, 0:t] - lam * o[:, t:2 * t]
    normed = out * lax.rsqrt(jnp.mean(out * out, axis=0, keepdims=True) + RMS_EPS)
    o_ref[...] = ((normed * g_ref[...]) * (1.0 - LAMBDA_INIT)).T.astype(BF16)


def _attention(qk, vt, qk_meta, vt_meta, lq1, lk1, lq2, lk2, subln, *, t=2048, tk=512, cq=2 * MXU_WIDTH,
               cast=(), steps_per_cast=2):
    m = qk.shape[0]
    lam_vec = pl.BlockSpec((1, HEAD_DIM), lambda h, i: (0, 0))
    casts_per_head = m // t // steps_per_cast
    cast_in, cast_out, cast_shapes = _cast_plumbing(
        cast, N_HEADS * casts_per_head, lambda h, i: (h * casts_per_head + i // steps_per_cast, 0))
    return pl.pallas_call(
        functools.partial(_attn_kernel, t=t, tk=tk, cq=cq, n_cast=len(cast), steps_per_cast=steps_per_cast),
        grid=(N_HEADS, m // t),
        in_specs=[
            pl.BlockSpec((t, V_DIM), lambda h, i: (i, h)),
            pl.BlockSpec((m, V_DIM), lambda h, i: (0, N_HEADS + h)),
            pl.BlockSpec((VT_ROWS, m), lambda h, i: (h, 0)),
            pl.BlockSpec((N_META, V_DIM), lambda h, i: (0, N_HEADS + h)),
            pl.BlockSpec((VT_ROWS, N_META), lambda h, i: (h, 0)),
            lam_vec, lam_vec, lam_vec, lam_vec,
            pl.BlockSpec((V_DIM, 1), lambda h, i: (0, 0)),
        ] + cast_in,
        out_specs=[pl.BlockSpec((t, V_DIM), lambda h, i: (i, h))] + cast_out,
        out_shape=[jax.ShapeDtypeStruct((m, ATT_WIDTH), BF16)] + cast_shapes,
        scratch_shapes=[pltpu.VMEM((2 * t, V_DIM), BF16), pltpu.VMEM((4, tk, cq), F32),
                        pltpu.VMEM((1, 2 * t), F32), pltpu.VMEM((VT_ROWS, 2 * t), F32)],
        compiler_params=_params("parallel", "arbitrary"),
        name="diff_attention",
    )(qk, qk, vt, qk_meta, vt_meta,
      lq1.reshape(1, HEAD_DIM), lk1.reshape(1, HEAD_DIM), lq2.reshape(1, HEAD_DIM), lk2.reshape(1, HEAD_DIM),
      subln.reshape(V_DIM, 1), *cast)


def _merge_kernel(h_ref, ya_ref, yc_ref, s_ref, wg0_ref, wg1_ref, wb0_ref, wb1_ref, wo_ref, o_ref, acc_scr):
    j = pl.program_id(1)

    @pl.when(j == 0)
    def _():
        acc_scr[...] = jnp.zeros_like(acc_scr)

    h = h_ref[...]
    gate_a = jax.nn.sigmoid(jnp.dot(h, wg0_ref[...], preferred_element_type=F32))
    gate_c = jax.nn.sigmoid(jnp.dot(h, wg1_ref[...], preferred_element_type=F32))
    z_a = jnp.dot(ya_ref[...], wb0_ref[...], preferred_element_type=F32)
    z_c = jnp.dot(yc_ref[...], wb1_ref[...], preferred_element_type=F32)
    merged = (gate_a * z_a + gate_c * z_c).astype(BF16)
    acc_scr[...] += jnp.dot(merged, wo_ref[...], preferred_element_type=F32)

    @pl.when(j == pl.num_programs(1) - 1)
    def _():
        o_ref[...] = s_ref[...] + acc_scr[...]


def _merge(h, y_att, y_conv, s, w_gate, w_branch, w_out, *, tm=512, tn=512):
    m = h.shape[0]
    n_d = D_MODEL // tn
    return pl.pallas_call(
        _merge_kernel,
        grid=(m // tm, n_d),
        in_specs=[
            pl.BlockSpec((tm, D_MODEL), lambda i, j: (i, 0)),
            pl.BlockSpec((tm, ATT_WIDTH), lambda i, j: (i, 0)),
            pl.BlockSpec((tm, CONV_WIDTH), lambda i, j: (i, 0)),
            pl.BlockSpec((tm, D_MODEL), lambda i, j: (i, 0)),
            pl.BlockSpec((D_MODEL, tn), lambda i, j: (0, j)),
            pl.BlockSpec((D_MODEL, tn), lambda i, j: (0, n_d + j)),
            pl.BlockSpec((None, ATT_WIDTH, tn), lambda i, j: (0, 0, j)),
            pl.BlockSpec((None, CONV_WIDTH, tn), lambda i, j: (1, 0, j)),
            pl.BlockSpec((tn, D_MODEL), lambda i, j: (j, 0)),
        ],
        out_specs=pl.BlockSpec((tm, D_MODEL), lambda i, j: (i, 0)),
        out_shape=jax.ShapeDtypeStruct((m, D_MODEL), F32),
        scratch_shapes=[pltpu.VMEM((tm, D_MODEL), F32)],
        compiler_params=_params("parallel", "arbitrary"),
        name="gated_merge",
    )(h, y_att, y_conv, s, w_gate, w_gate, w_branch, w_branch, w_out)


def kernel(x, meta_tokens, norm_ffn1, ffn1_w_gu, ffn1_w_down, norm_mix, w_in, conv_w, lambda_q1, lambda_k1,
           lambda_q2, lambda_k2, subln, w_branch, w_gate, w_out, norm_ffn2, ffn2_w_gu, ffn2_w_down, norm_final):
    bsz, seq, _ = x.shape
    assert bsz == 1 and meta_tokens.shape == (N_META, D_MODEL)
    xs = x.reshape(seq, D_MODEL)
    _, h_meta, w_gate1, w_up1, w_down1 = _ffn(
        meta_tokens.astype(x.dtype), norm_ffn1[0], ffn1_w_gu[0], ffn1_w_gu[0], ffn1_w_down[0], norm_mix[0],
        final=False, tm=N_META, emit_weights=True)
    s1, h, w_in_bf = _ffn(xs, norm_ffn1[0], w_gate1, w_up1, w_down1, norm_mix[0], final=False, tm=512,
                          cast=(w_in[0],))

    qk_meta, v_meta, _, meta_tail = _mixer_proj(h_meta, w_in_bf, conv_w[0], jnp.zeros((SUBLANES, CONV_WIDTH), F32),
                                                tm=N_META, transpose_v=False)
    qk, vt, y_conv, _ = _mixer_proj(h, w_in_bf, conv_w[0], meta_tail, tm=512, transpose_v=True)

    later_weights = (w_gate[0], w_branch[0].reshape(N_BRANCH * ATT_WIDTH, D_MODEL), w_out[0],
                     ffn2_w_gu[0], ffn2_w_down[0])
    vt_meta = jnp.concatenate(
        [v_meta.reshape(N_META, N_HEADS, V_DIM).transpose(1, 2, 0),
         jnp.zeros((N_HEADS, BF16_ROWS, N_META), BF16).at[:, 0, :].set(1)], axis=1).reshape(-1, N_META)
    y_att, w_gate_bf, w_branch_bf, w_out_bf, w_gu2, w_d2 = _attention(
        qk, vt, qk_meta, vt_meta, lambda_q1[0], lambda_k1[0], lambda_q2[0], lambda_k2[0], subln[0],
        cast=later_weights)
    s2 = _merge(h, y_att, y_conv, s1, w_gate_bf, w_branch_bf.reshape(N_BRANCH, ATT_WIDTH, D_MODEL), w_out_bf)
    out, = _ffn(s2, norm_ffn2[0], w_gu2, w_gu2, w_d2, norm_final, final=True, tm=512)
    return out.reshape(bsz, seq, D_MODEL)
```

```python
import functools
import math

import jax
import jax.numpy as jnp
from jax import lax
from jax.experimental import pallas as pl
from jax.experimental.pallas import tpu as pltpu

D_MODEL = 2048
N_META = 16
HEAD_DIM = 64
V_DIM = 2 * HEAD_DIM
ATT_WIDTH = D_MODEL // 2
N_HEADS = ATT_WIDTH // V_DIM
CONV_WIDTH = D_MODEL // 2
CONV_K = 3
N_BRANCH = 2
D_FF = ((8 * D_MODEL // 3 + 255) // 256) * 256
RMS_EPS = 1e-6
LAMBDA_INIT = 0.8 - 0.6 * math.exp(-0.3 * 0)
MASKED = -1e30
Q_SCALE = math.log2(math.e) * HEAD_DIM ** -0.5

SUBLANES = 8
BF16_ROWS = 16
VT_ROWS = V_DIM + BF16_ROWS
MXU_WIDTH = 256
VMEM_LIMIT_BYTES = 56 * 1024 * 1024

BF16 = jnp.bfloat16
F32 = jnp.float32
NT_DIMS = (((1,), (1,)), ((), ()))


def _rms_scale(x):
    return x * lax.rsqrt(jnp.mean(x * x, axis=-1, keepdims=True) + RMS_EPS)


def _params(*semantics):
    return pltpu.CompilerParams(dimension_semantics=semantics, vmem_limit_bytes=VMEM_LIMIT_BYTES)


def _cast_plumbing(weights, n_blocks, index_map):
    specs, shapes = [], []
    for w in weights:
        rows, cols = w.shape
        assert rows % (n_blocks * BF16_ROWS) == 0, (w.shape, n_blocks)
        specs.append(pl.BlockSpec((rows // n_blocks, cols), index_map))
        shapes.append(jax.ShapeDtypeStruct(w.shape, BF16))
    return specs, list(specs), shapes


def _cast_blocks(src_refs, dst_refs):
    for src, dst in zip(src_refs, dst_refs):
        dst[...] = src[...].astype(BF16)


def _ffn_kernel(x_ref, g_ref, wg_ref, wu_ref, wd_ref, gn_ref, *refs, final, n_cast, emit_weights):
    cast_in, refs = refs[:n_cast], refs[n_cast:]
    n_out = 1 if final else 2
    outs, cast_out, refs = refs[:n_out], refs[n_out:n_out + n_cast], refs[n_out + n_cast:]
    own_out, (h_scr,) = (refs[:3], refs[3:]) if emit_weights else ((), refs)
    acc_ref = outs[0]
    j, last = pl.program_id(1), pl.num_programs(1) - 1
    tm = h_scr.shape[0]
    halves = [slice(0, tm // 2), slice(tm // 2, tm)] if tm % (2 * MXU_WIDTH) == 0 else [slice(0, tm)]

    def weights():
        ws = [ref[...].astype(BF16) for ref in (wg_ref, wu_ref, wd_ref)]
        for dst, w in zip(own_out, ws):
            dst[...] = w
        return ws

    def gate_up(h, w_gate, w_up):
        return jnp.dot(h, w_gate, preferred_element_type=F32), jnp.dot(h, w_up, preferred_element_type=F32)

    def down(gate, up, w_down):
        act = (gate * jax.nn.sigmoid(gate) * up).astype(BF16)
        return jnp.dot(act, w_down, preferred_element_type=F32)

    @pl.when(j == 0)
    def _():
        _cast_blocks(cast_in, cast_out)
        w_gate, w_up, w_down = weights()
        gate_ups = []
        for rows in halves:
            h = (_rms_scale(x_ref[rows, :]) * g_ref[...]).astype(BF16)
            h_scr[rows, :] = h
            gate_ups.append(gate_up(h, w_gate, w_up))
        for rows, (gate, up) in zip(halves, gate_ups):
            acc_ref[rows, :] = down(gate, up, w_down)

    @pl.when(jnp.logical_and(j > 0, j < last))
    def _():
        w_gate, w_up, w_down = weights()
        acc_ref[...] += down(*gate_up(h_scr[...], w_gate, w_up), w_down)

    @pl.when(j == last)
    def _():
        w_gate, w_up, w_down = weights()
        gate_ups = [gate_up(h_scr[rows, :], w_gate, w_up) for rows in halves]
        for rows, (gate, up) in zip(halves, gate_ups):
            s = x_ref[rows, :] + 0.5 * (acc_ref[rows, :] + down(gate, up, w_down))
            normed = _rms_scale(s) * gn_ref[...]
            if final:
                outs[0][rows, :] = normed
            else:
                outs[0][rows, :] = s
                outs[1][rows, :] = normed.astype(BF16)


def _ffn(x, g, w_gate, w_up, w_down, g_next, *, final, tm, tf=512, cast=(), emit_weights=False):
    m = x.shape[0]
    n_f = D_FF // tf
    assert n_f >= 2
    up_block = w_up.shape[1] // tf - n_f
    row = pl.BlockSpec((tm, D_MODEL), lambda i, j: (i, 0))
    vec = pl.BlockSpec((1, D_MODEL), lambda i, j: (0, 0))
    gate_or_up = pl.BlockSpec((D_MODEL, tf), lambda i, j: (0, j))
    down = pl.BlockSpec((tf, D_MODEL), lambda i, j: (j, 0))
    out_shapes = [jax.ShapeDtypeStruct((m, D_MODEL), F32)]
    if not final:
        out_shapes.append(jax.ShapeDtypeStruct((m, D_MODEL), BF16))
    out_specs = [row] * len(out_shapes)
    cast_in, cast_out, cast_shapes = _cast_plumbing(cast, m // tm, lambda i, j: (i, 0))
    if emit_weights:
        assert m == tm
        cast_out = cast_out + [gate_or_up, gate_or_up, down]
        cast_shapes = cast_shapes + [jax.ShapeDtypeStruct(s, BF16) for s in
                                     ((D_MODEL, D_FF), (D_MODEL, D_FF), (D_FF, D_MODEL))]
    return pl.pallas_call(
        functools.partial(_ffn_kernel, final=final, n_cast=len(cast), emit_weights=emit_weights),
        grid=(m // tm, n_f),
        in_specs=[row, vec, gate_or_up, pl.BlockSpec((D_MODEL, tf), lambda i, j: (0, up_block + j)), down, vec]
                 + cast_in,
        out_specs=out_specs + cast_out,
        out_shape=out_shapes + cast_shapes,
        scratch_shapes=[pltpu.VMEM((tm, D_MODEL), BF16)],
        compiler_params=_params("parallel", "arbitrary"),
        name="ffn_final" if final else "ffn_mix",
    )(x, g.reshape(1, D_MODEL), w_gate, w_up, w_down, g_next.reshape(1, D_MODEL), *cast)


def _mixer_proj_kernel(h_ref, w_ref, cw_ref, cin_ref, qk_ref, v_ref, y_ref, cout_ref, carry_scr, *, transpose_v):
    i = pl.program_id(0)
    tm = h_ref.shape[0]

    @pl.when(i == 0)
    def _():
        carry_scr[...] = cin_ref[...]

    h = h_ref[...]

    def proj(first_col, width):
        return jnp.dot(h, w_ref[:, first_col:first_col + width], preferred_element_type=F32)

    qk_ref[:, 0:ATT_WIDTH] = (proj(0, ATT_WIDTH) * Q_SCALE).astype(BF16)
    qk_ref[:, ATT_WIDTH:2 * ATT_WIDTH] = proj(ATT_WIDTH, ATT_WIDTH).astype(BF16)
    v_proj = proj(2 * ATT_WIDTH, ATT_WIDTH)
    if transpose_v:
        v_t = v_proj.T
        ones_then_zeros = (lax.broadcasted_iota(jnp.int32, (BF16_ROWS, tm), 0) == 0).astype(BF16)
        for head in range(N_HEADS):
            v_ref[head * VT_ROWS:head * VT_ROWS + V_DIM, :] = v_t[head * V_DIM:(head + 1) * V_DIM, :].astype(BF16)
            v_ref[head * VT_ROWS + V_DIM:(head + 1) * VT_ROWS, :] = ones_then_zeros
    else:
        v_ref[...] = v_proj.astype(BF16)

    c0 = 3 * ATT_WIDTH
    b_gate = proj(c0, CONV_WIDTH)
    v = proj(c0 + CONV_WIDTH, CONV_WIDTH) * proj(c0 + 2 * CONV_WIDTH, CONV_WIDTH)
    w0, w1, w2 = cw_ref[0:1, :], cw_ref[1:2, :], cw_ref[2:3, :]

    y = w0 * pltpu.roll(v, 2, axis=0) + w1 * pltpu.roll(v, 1, axis=0) + w2 * v
    y_ref[...] = (b_gate * y).astype(BF16)

    prev = carry_scr[...]
    head = v[0:SUBLANES, :]
    rows = lax.broadcasted_iota(jnp.int32, head.shape, 0)
    v1 = jnp.where(rows >= 1, pltpu.roll(head, 1, axis=0), pltpu.roll(prev, 1, axis=0))
    v2 = jnp.where(rows >= 2, pltpu.roll(head, 2, axis=0), pltpu.roll(prev, 2, axis=0))
    y_head = w0 * v2 + w1 * v1 + w2 * head
    y_ref[0:SUBLANES, :] = (b_gate[0:SUBLANES, :] * y_head).astype(BF16)

    tail = v[tm - SUBLANES:tm, :]
    carry_scr[...] = tail
    cout_ref[...] = tail


def _mixer_proj(h, w_in, conv_w, carry_in, *, tm, transpose_v):
    m = h.shape[0]
    whole = functools.partial(pl.BlockSpec, index_map=lambda i: (0, 0), pipeline_mode=pl.Buffered(1))
    if transpose_v:
        v_spec, v_shape = pl.BlockSpec((N_HEADS * VT_ROWS, tm), lambda i: (0, i)), (N_HEADS * VT_ROWS, m)
    else:
        v_spec, v_shape = pl.BlockSpec((tm, ATT_WIDTH), lambda i: (i, 0)), (m, ATT_WIDTH)
    return pl.pallas_call(
        functools.partial(_mixer_proj_kernel, transpose_v=transpose_v),
        grid=(m // tm,),
        in_specs=[pl.BlockSpec((tm, D_MODEL), lambda i: (i, 0)),
                  whole(w_in.shape), whole((CONV_K, CONV_WIDTH)), whole((SUBLANES, CONV_WIDTH))],
        out_specs=(pl.BlockSpec((tm, 2 * ATT_WIDTH), lambda i: (i, 0)),
                   v_spec,
                   pl.BlockSpec((tm, CONV_WIDTH), lambda i: (i, 0)),
                   pl.BlockSpec((SUBLANES, CONV_WIDTH), lambda i: (i, 0))),
        out_shape=(jax.ShapeDtypeStruct((m, 2 * ATT_WIDTH), BF16),
                   jax.ShapeDtypeStruct(v_shape, BF16),
                   jax.ShapeDtypeStruct((m, CONV_WIDTH), BF16),
                   jax.ShapeDtypeStruct((m // tm * SUBLANES, CONV_WIDTH), F32)),
        scratch_shapes=[pltpu.VMEM((SUBLANES, CONV_WIDTH), F32)],
        compiler_params=_params("arbitrary"),
        name="mixer_proj",
    )(h, w_in, conv_w, carry_in)


def _attn_kernel(q_ref, k_ref, vt_ref, km_ref, vtm_ref, lq1_ref, lk1_ref, lq2_ref, lk2_ref, g_ref, *refs,
                 t, tk, cq, n_cast, steps_per_cast):
    cast_in, o_ref, cast_out = refs[:n_cast], refs[n_cast], refs[n_cast + 1:2 * n_cast + 1]
    q2_scr, s_scr, m_scr, acc_scr = refs[2 * n_cast + 1:]
    i = pl.program_id(1)
    n_chunks = 2 * t // cq

    @pl.when(i % steps_per_cast == 0)
    def _():
        _cast_blocks(cast_in, cast_out)

    q = q_ref[...]
    lane = lax.broadcasted_iota(jnp.int32, q.shape, 1)
    zero = jnp.zeros_like(q)
    q2_scr[0:t, :] = jnp.where(lane < HEAD_DIM, q, zero)
    q2_scr[t:2 * t, :] = jnp.where(lane >= HEAD_DIM, q, zero)

    def scores(k, c):
        return lax.dot_general(k, q2_scr[c * cq:(c + 1) * cq, :], NT_DIMS, preferred_element_type=F32)

    def keys_of(base, item):
        if item[0] is None:
            return km_ref[...], vtm_ref[...]
        return k_ref[pl.ds(base + item[0], tk), :], vt_ref[:, pl.ds(base + item[0], tk)]

    def chunk_update(c, s_ref, vt, key_offset):
        cols = slice(c * cq, (c + 1) * cq)

        def load_scores():
            s = s_ref[...]
            if key_offset is None:
                return s
            key = key_offset + lax.broadcasted_iota(jnp.int32, s.shape, 0)
            query = (c * cq) % t + lax.broadcasted_iota(jnp.int32, s.shape, 1)
            return jnp.where(key <= query, s, MASKED)

        m_old = m_scr[:, cols]
        m_new = jnp.maximum(m_old, jnp.max(load_scores(), axis=0, keepdims=True))
        alpha = jnp.exp2(m_old - m_new)
        p = jnp.exp2(load_scores() - m_new)
        m_scr[:, cols] = m_new
        acc_scr[:, cols] = alpha * acc_scr[:, cols] + jnp.dot(vt, p.astype(BF16), preferred_element_type=F32)

    n_slots = s_scr.shape[0]
    ahead = n_slots // 2

    def issue_scores(base, item, slot):
        k = keys_of(base, item)[0]
        s_scr[slot, 0:k.shape[0], :] = scores(k, item[1])

    def run(base, items, next_items):
        assert len(items) % n_slots == 0 or not next_items
        upcoming = items + next_items
        for n, item in enumerate(items):
            if n + ahead < len(upcoming):
                issue_scores(base, upcoming[n + ahead], (n + ahead) % n_slots)
            vt = keys_of(base, item)[1]
            chunk_update(item[1], s_scr.at[n % n_slots, 0:vt.shape[1], :], vt, item[2])

    visible_items = [(d * tk, c, None) for d in range(t // tk) for c in range(n_chunks)]
    diagonal_items = []
    for d in range(t // tk):
        for c in range(n_chunks):
            first_query = (c * cq) % t
            if d * tk <= first_query + cq - 1:
                fully_visible = d * tk + tk - 1 <= first_query
                assert d * tk <= first_query
                diagonal_items.append((d * tk, c, None if fully_visible else d * tk))
    diagonal_items += [(None, c, None) for c in range(n_chunks)]

    m_scr[...] = jnp.full_like(m_scr, MASKED)
    acc_scr[...] = jnp.zeros_like(acc_scr)
    assert [item[:2] for item in diagonal_items[:ahead]] == [item[:2] for item in visible_items[:ahead]]
    for n in range(ahead):
        issue_scores(0, visible_items[n], n)

    def body(g, carry):
        run(pl.multiple_of(g * t, t), visible_items, [(t + rel, c, None) for rel, c, _ in visible_items[:ahead]])
        return carry

    lax.fori_loop(0, i, body, 0)
    run(pl.multiple_of(i * t, t), diagonal_items, [])

    lam = (jnp.exp(jnp.sum(lq1_ref[...] * lk1_ref[...], axis=-1, keepdims=True))
           - jnp.exp(jnp.sum(lq2_ref[...] * lk2_ref[...], axis=-1, keepdims=True)) + LAMBDA_INIT)
    o = acc_scr[0:V_DIM, :] / acc_scr[V_DIM:V_DIM + 1, :]
    out = o[:, 0:t] - lam * o[:, t:2 * t]
    normed = out * lax.rsqrt(jnp.mean(out * out, axis=0, keepdims=True) + RMS_EPS)
    o_ref[...] = ((normed * g_ref[...]) * (1.0 - LAMBDA_INIT)).T.astype(BF16)


def _attention(qk, vt, qk_meta, vt_meta, lq1, lk1, lq2, lk2, subln, *, t=2048, tk=512, cq=2 * MXU_WIDTH,
               cast=(), steps_per_cast=2):
    m = qk.shape[0]
    lam_vec = pl.BlockSpec((1, HEAD_DIM), lambda h, i: (0, 0))
    casts_per_head = m // t // steps_per_cast
    cast_in, cast_out, cast_shapes = _cast_plumbing(
        cast, N_HEADS * casts_per_head, lambda h, i: (h * casts_per_head + i // steps_per_cast, 0))
    return pl.pallas_call(
        functools.partial(_attn_kernel, t=t, tk=tk, cq=cq, n_cast=len(cast), steps_per_cast=steps_per_cast),
        grid=(N_HEADS, m // t),
        in_specs=[
            pl.BlockSpec((t, V_DIM), lambda h, i: (i, h)),
            pl.BlockSpec((m, V_DIM), lambda h, i: (0, N_HEADS + h)),
            pl.BlockSpec((VT_ROWS, m), lambda h, i: (h, 0)),
            pl.BlockSpec((N_META, V_DIM), lambda h, i: (0, N_HEADS + h)),
            pl.BlockSpec((VT_ROWS, N_META), lambda h, i: (h, 0)),
            lam_vec, lam_vec, lam_vec, lam_vec,
            pl.BlockSpec((V_DIM, 1), lambda h, i: (0, 0)),
        ] + cast_in,
        out_specs=[pl.BlockSpec((t, V_DIM), lambda h, i: (i, h))] + cast_out,
        out_shape=[jax.ShapeDtypeStruct((m, ATT_WIDTH), BF16)] + cast_shapes,
        scratch_shapes=[pltpu.VMEM((2 * t, V_DIM), BF16), pltpu.VMEM((4, tk, cq), F32),
                        pltpu.VMEM((1, 2 * t), F32), pltpu.VMEM((VT_ROWS, 2 * t), F32)],
        compiler_params=_params("parallel", "arbitrary"),
        name="diff_attention",
    )(qk, qk, vt, qk_meta, vt_meta,
      lq1.reshape(1, HEAD_DIM), lk1.reshape(1, HEAD_DIM), lq2.reshape(1, HEAD_DIM), lk2.reshape(1, HEAD_DIM),
      subln.reshape(V_DIM, 1), *cast)


def _merge_kernel(h_ref, ya_ref, yc_ref, s_ref, wg0_ref, wg1_ref, wb0_ref, wb1_ref, wo_ref, o_ref, acc_scr):
    j = pl.program_id(1)

    @pl.when(j == 0)
    def _():
        acc_scr[...] = jnp.zeros_like(acc_scr)

    h = h_ref[...]
    gate_a = jax.nn.sigmoid(jnp.dot(h, wg0_ref[...], preferred_element_type=F32))
    gate_c = jax.nn.sigmoid(jnp.dot(h, wg1_ref[...], preferred_element_type=F32))
    z_a = jnp.dot(ya_ref[...], wb0_ref[...], preferred_element_type=F32)
    z_c = jnp.dot(yc_ref[...], wb1_ref[...], preferred_element_type=F32)
    merged = (gate_a * z_a + gate_c * z_c).astype(BF16)
    acc_scr[...] += jnp.dot(merged, wo_ref[...], preferred_element_type=F32)

    @pl.when(j == pl.num_programs(1) - 1)
    def _():
        o_ref[...] = s_ref[...] + acc_scr[...]


def _merge(h, y_att, y_conv, s, w_gate, w_branch, w_out, *, tm=512, tn=512):
    m = h.shape[0]
    n_d = D_MODEL // tn
    return pl.pallas_call(
        _merge_kernel,
        grid=(m // tm, n_d),
        in_specs=[
            pl.BlockSpec((tm, D_MODEL), lambda i, j: (i, 0)),
            pl.BlockSpec((tm, ATT_WIDTH), lambda i, j: (i, 0)),
            pl.BlockSpec((tm, CONV_WIDTH), lambda i, j: (i, 0)),
            pl.BlockSpec((tm, D_MODEL), lambda i, j: (i, 0)),
            pl.BlockSpec((D_MODEL, tn), lambda i, j: (0, j)),
            pl.BlockSpec((D_MODEL, tn), lambda i, j: (0, n_d + j)),
            pl.BlockSpec((None, ATT_WIDTH, tn), lambda i, j: (0, 0, j)),
            pl.BlockSpec((None, CONV_WIDTH, tn), lambda i, j: (1, 0, j)),
            pl.BlockSpec((tn, D_MODEL), lambda i, j: (j, 0)),
        ],
        out_specs=pl.BlockSpec((tm, D_MODEL), lambda i, j: (i, 0)),
        out_shape=jax.ShapeDtypeStruct((m, D_MODEL), F32),
        scratch_shapes=[pltpu.VMEM((tm, D_MODEL), F32)],
        compiler_params=_params("parallel", "arbitrary"),
        name="gated_merge",
    )(h, y_att, y_conv, s, w_gate, w_gate, w_branch, w_branch, w_out)


def kernel(x, meta_tokens, norm_ffn1, ffn1_w_gu, ffn1_w_down, norm_mix, w_in, conv_w, lambda_q1, lambda_k1,
           lambda_q2, lambda_k2, subln, w_branch, w_gate, w_out, norm_ffn2, ffn2_w_gu, ffn2_w_down, norm_final):
    bsz, seq, _ = x.shape
    assert bsz == 1 and meta_tokens.shape == (N_META, D_MODEL)
    xs = x.reshape(seq, D_MODEL)
    _, h_meta, w_gate1, w_up1, w_down1 = _ffn(
        meta_tokens.astype(x.dtype), norm_ffn1[0], ffn1_w_gu[0], ffn1_w_gu[0], ffn1_w_down[0], norm_mix[0],
        final=False, tm=N_META, emit_weights=True)
    s1, h, w_in_bf = _ffn(xs, norm_ffn1[0], w_gate1, w_up1, w_down1, norm_mix[0], final=False, tm=512,
                          cast=(w_in[0],))

    qk_meta, v_meta, _, meta_tail = _mixer_proj(h_meta, w_in_bf, conv_w[0], jnp.zeros((SUBLANES, CONV_WIDTH), F32),
                                                tm=N_META, transpose_v=False)
    qk, vt, y_conv, _ = _mixer_proj(h, w_in_bf, conv_w[0], meta_tail, tm=512, transpose_v=True)

    later_weights = (w_gate[0], w_branch[0].reshape(N_BRANCH * ATT_WIDTH, D_MODEL), w_out[0],
                     ffn2_w_gu[0], ffn2_w_down[0])
    vt_meta = jnp.concatenate(
        [v_meta.reshape(N_META, N_HEADS, V_DIM).transpose(1, 2, 0),
         jnp.zeros((N_HEADS, BF16_ROWS, N_META), BF16).at[:, 0, :].set(1)], axis=1).reshape(-1, N_META)
    y_att, w_gate_bf, w_branch_bf, w_out_bf, w_gu2, w_d2 = _attention(
        qk, vt, qk_meta, vt_meta, lambda_q1[0], lambda_k1[0], lambda_q2[0], lambda_k2[0], subln[0],
        cast=later_weights)
    s2 = _merge(h, y_att, y_conv, s1, w_gate_bf, w_branch_bf.reshape(N_BRANCH, ATT_WIDTH, D_MODEL), w_out_bf)
    out, = _ffn(s2, norm_ffn2[0], w_gu2, w_gu2, w_d2, norm_final, final=True, tm=1024)
    return out.reshape(bsz, seq, D_MODEL)
```

```python
import functools
import math

import jax
import jax.numpy as jnp
from jax import lax
from jax.experimental import pallas as pl
from jax.experimental.pallas import tpu as pltpu

D_MODEL = 2048
N_META = 16
HEAD_DIM = 64
V_DIM = 2 * HEAD_DIM
ATT_WIDTH = D_MODEL // 2
N_HEADS = ATT_WIDTH // V_DIM
CONV_WIDTH = D_MODEL // 2
CONV_K = 3
N_BRANCH = 2
D_FF = ((8 * D_MODEL // 3 + 255) // 256) * 256
RMS_EPS = 1e-6
LAMBDA_INIT = 0.8 - 0.6 * math.exp(-0.3 * 0)
MASKED = -1e30
Q_SCALE = math.log2(math.e) * HEAD_DIM ** -0.5

SUBLANES = 8
BF16_ROWS = 16
VT_ROWS = V_DIM + BF16_ROWS
MXU_WIDTH = 256
VMEM_LIMIT_BYTES = 56 * 1024 * 1024

BF16 = jnp.bfloat16
F32 = jnp.float32
NT_DIMS = (((1,), (1,)), ((), ()))


def _rms_scale(x):
    return x * lax.rsqrt(jnp.mean(x * x, axis=-1, keepdims=True) + RMS_EPS)


def _params(*semantics):
    return pltpu.CompilerParams(dimension_semantics=semantics, vmem_limit_bytes=VMEM_LIMIT_BYTES)


def _cast_plumbing(weights, n_blocks, index_map):
    specs, shapes = [], []
    for w in weights:
        rows, cols = w.shape
        assert rows % (n_blocks * BF16_ROWS) == 0, (w.shape, n_blocks)
        specs.append(pl.BlockSpec((rows // n_blocks, cols), index_map))
        shapes.append(jax.ShapeDtypeStruct(w.shape, BF16))
    return specs, list(specs), shapes


def _cast_blocks(src_refs, dst_refs):
    for src, dst in zip(src_refs, dst_refs):
        dst[...] = src[...].astype(BF16)


def _ffn_kernel(x_ref, g_ref, wg_ref, wu_ref, wd_ref, gn_ref, *refs, final, n_cast, emit_weights):
    cast_in, refs = refs[:n_cast], refs[n_cast:]
    n_out = 1 if final else 2
    outs, cast_out, refs = refs[:n_out], refs[n_out:n_out + n_cast], refs[n_out + n_cast:]
    own_out, (h_scr,) = (refs[:3], refs[3:]) if emit_weights else ((), refs)
    acc_ref = outs[0]
    j, last = pl.program_id(1), pl.num_programs(1) - 1
    tm = h_scr.shape[0]
    halves = [slice(0, tm // 2), slice(tm // 2, tm)] if tm % (2 * MXU_WIDTH) == 0 else [slice(0, tm)]

    def weights():
        ws = [ref[...].astype(BF16) for ref in (wg_ref, wu_ref, wd_ref)]
        for dst, w in zip(own_out, ws):
            dst[...] = w
        return ws

    def gate_up(h, w_gate, w_up):
        return jnp.dot(h, w_gate, preferred_element_type=F32), jnp.dot(h, w_up, preferred_element_type=F32)

    def down(gate, up, w_down):
        act = (gate * jax.nn.sigmoid(gate) * up).astype(BF16)
        return jnp.dot(act, w_down, preferred_element_type=F32)

    @pl.when(j == 0)
    def _():
        _cast_blocks(cast_in, cast_out)
        w_gate, w_up, w_down = weights()
        gate_ups = []
        for rows in halves:
            h = (_rms_scale(x_ref[rows, :]) * g_ref[...]).astype(BF16)
            h_scr[rows, :] = h
            gate_ups.append(gate_up(h, w_gate, w_up))
        for rows, (gate, up) in zip(halves, gate_ups):
            acc_ref[rows, :] = down(gate, up, w_down)

    @pl.when(jnp.logical_and(j > 0, j < last))
    def _():
        w_gate, w_up, w_down = weights()
        acc_ref[...] += down(*gate_up(h_scr[...], w_gate, w_up), w_down)

    @pl.when(j == last)
    def _():
        w_gate, w_up, w_down = weights()
        gate_ups = [gate_up(h_scr[rows, :], w_gate, w_up) for rows in halves]
        for rows, (gate, up) in zip(halves, gate_ups):
            s = x_ref[rows, :] + 0.5 * (acc_ref[rows, :] + down(gate, up, w_down))
            normed = _rms_scale(s) * gn_ref[...]
            if final:
                outs[0][rows, :] = normed
            else:
                outs[0][rows, :] = s
                outs[1][rows, :] = normed.astype(BF16)


def _ffn(x, g, w_gate, w_up, w_down, g_next, *, final, tm, tf=512, cast=(), emit_weights=False):
    m = x.shape[0]
    n_f = D_FF // tf
    assert n_f >= 2
    up_block = w_up.shape[1] // tf - n_f
    row = pl.BlockSpec((tm, D_MODEL), lambda i, j: (i, 0))
    vec = pl.BlockSpec((1, D_MODEL), lambda i, j: (0, 0))
    gate_or_up = pl.BlockSpec((D_MODEL, tf), lambda i, j: (0, j))
    down = pl.BlockSpec((tf, D_MODEL), lambda i, j: (j, 0))
    out_shapes = [jax.ShapeDtypeStruct((m, D_MODEL), F32)]
    if not final:
        out_shapes.append(jax.ShapeDtypeStruct((m, D_MODEL), BF16))
    out_specs = [row] * len(out_shapes)
    cast_in, cast_out, cast_shapes = _cast_plumbing(cast, m // tm, lambda i, j: (i, 0))
    if emit_weights:
        assert m == tm
        cast_out = cast_out + [gate_or_up, gate_or_up, down]
        cast_shapes = cast_shapes + [jax.ShapeDtypeStruct(s, BF16) for s in
                                     ((D_MODEL, D_FF), (D_MODEL, D_FF), (D_FF, D_MODEL))]
    return pl.pallas_call(
        functools.partial(_ffn_kernel, final=final, n_cast=len(cast), emit_weights=emit_weights),
        grid=(m // tm, n_f),
        in_specs=[row, vec, gate_or_up, pl.BlockSpec((D_MODEL, tf), lambda i, j: (0, up_block + j)), down, vec]
                 + cast_in,
        out_specs=out_specs + cast_out,
        out_shape=out_shapes + cast_shapes,
        scratch_shapes=[pltpu.VMEM((tm, D_MODEL), BF16)],
        compiler_params=_params("parallel", "arbitrary"),
        name="ffn_final" if final else "ffn_mix",
    )(x, g.reshape(1, D_MODEL), w_gate, w_up, w_down, g_next.reshape(1, D_MODEL), *cast)


def _mixer_proj_kernel(h_ref, w_ref, cw_ref, cin_ref, qk_ref, v_ref, y_ref, cout_ref, carry_scr, *, transpose_v):
    i = pl.program_id(0)
    tm = h_ref.shape[0]

    @pl.when(i == 0)
    def _():
        carry_scr[...] = cin_ref[...]

    h = h_ref[...]

    def proj(first_col, width):
        return jnp.dot(h, w_ref[:, first_col:first_col + width], preferred_element_type=F32)

    qk_ref[:, 0:ATT_WIDTH] = (proj(0, ATT_WIDTH) * Q_SCALE).astype(BF16)
    qk_ref[:, ATT_WIDTH:2 * ATT_WIDTH] = proj(ATT_WIDTH, ATT_WIDTH).astype(BF16)
    v_proj = proj(2 * ATT_WIDTH, ATT_WIDTH)
    if transpose_v:
        v_t = v_proj.T
        ones_then_zeros = (lax.broadcasted_iota(jnp.int32, (BF16_ROWS, tm), 0) == 0).astype(BF16)
        for head in range(N_HEADS):
            v_ref[head * VT_ROWS:head * VT_ROWS + V_DIM, :] = v_t[head * V_DIM:(head + 1) * V_DIM, :].astype(BF16)
            v_ref[head * VT_ROWS + V_DIM:(head + 1) * VT_ROWS, :] = ones_then_zeros
    else:
        v_ref[...] = v_proj.astype(BF16)

    c0 = 3 * ATT_WIDTH
    b_gate = proj(c0, CONV_WIDTH)
    v = proj(c0 + CONV_WIDTH, CONV_WIDTH) * proj(c0 + 2 * CONV_WIDTH, CONV_WIDTH)
    w0, w1, w2 = cw_ref[0:1, :], cw_ref[1:2, :], cw_ref[2:3, :]

    y = w0 * pltpu.roll(v, 2, axis=0) + w1 * pltpu.roll(v, 1, axis=0) + w2 * v
    y_ref[...] = (b_gate * y).astype(BF16)

    prev = carry_scr[...]
    head = v[0:SUBLANES, :]
    rows = lax.broadcasted_iota(jnp.int32, head.shape, 0)
    v1 = jnp.where(rows >= 1, pltpu.roll(head, 1, axis=0), pltpu.roll(prev, 1, axis=0))
    v2 = jnp.where(rows >= 2, pltpu.roll(head, 2, axis=0), pltpu.roll(prev, 2, axis=0))
    y_head = w0 * v2 + w1 * v1 + w2 * head
    y_ref[0:SUBLANES, :] = (b_gate[0:SUBLANES, :] * y_head).astype(BF16)

    tail = v[tm - SUBLANES:tm, :]
    carry_scr[...] = tail
    cout_ref[...] = tail


def _mixer_proj(h, w_in, conv_w, carry_in, *, tm, transpose_v):
    m = h.shape[0]
    whole = functools.partial(pl.BlockSpec, index_map=lambda i: (0, 0), pipeline_mode=pl.Buffered(1))
    if transpose_v:
        v_spec, v_shape = pl.BlockSpec((N_HEADS * VT_ROWS, tm), lambda i: (0, i)), (N_HEADS * VT_ROWS, m)
    else:
        v_spec, v_shape = pl.BlockSpec((tm, ATT_WIDTH), lambda i: (i, 0)), (m, ATT_WIDTH)
    return pl.pallas_call(
        functools.partial(_mixer_proj_kernel, transpose_v=transpose_v),
        grid=(m // tm,),
        in_specs=[pl.BlockSpec((tm, D_MODEL), lambda i: (i, 0)),
                  whole(w_in.shape), whole((CONV_K, CONV_WIDTH)), whole((SUBLANES, CONV_WIDTH))],
        out_specs=(pl.BlockSpec((tm, 2 * ATT_WIDTH), lambda i: (i, 0)),
                   v_spec,
                   pl.BlockSpec((tm, CONV_WIDTH), lambda i: (i, 0)),
                   pl.BlockSpec((SUBLANES, CONV_WIDTH), lambda i: (i, 0))),
        out_shape=(jax.ShapeDtypeStruct((m, 2 * ATT_WIDTH), BF16),
                   jax.ShapeDtypeStruct(v_shape, BF16),
                   jax.ShapeDtypeStruct((m, CONV_WIDTH), BF16),
                   jax.ShapeDtypeStruct((m // tm * SUBLANES, CONV_WIDTH), F32)),
        scratch_shapes=[pltpu.VMEM((SUBLANES, CONV_WIDTH), F32)],
        compiler_params=_params("arbitrary"),
        name="mixer_proj",
    )(h, w_in, conv_w, carry_in)


def _attn_kernel(q_ref, k_ref, vt_ref, km_ref, vtm_ref, lq1_ref, lk1_ref, lq2_ref, lk2_ref, g_ref, *refs,
                 t, tk, cq, n_cast, steps_per_cast):
    cast_in, o_ref, cast_out = refs[:n_cast], refs[n_cast], refs[n_cast + 1:2 * n_cast + 1]
    q2_scr, s_scr, m_scr, acc_scr = refs[2 * n_cast + 1:]
    i = pl.program_id(1)
    n_chunks = 2 * t // cq

    @pl.when(i % steps_per_cast == 0)
    def _():
        _cast_blocks(cast_in, cast_out)

    q = q_ref[...]
    lane = lax.broadcasted_iota(jnp.int32, q.shape, 1)
    zero = jnp.zeros_like(q)
    q2_scr[0:t, :] = jnp.where(lane < HEAD_DIM, q, zero)
    q2_scr[t:2 * t, :] = jnp.where(lane >= HEAD_DIM, q, zero)

    def scores(k, c):
        return lax.dot_general(k, q2_scr[c * cq:(c + 1) * cq, :], NT_DIMS, preferred_element_type=F32)

    def keys_of(base, item):
        if item[0] is None:
            return km_ref[...], vtm_ref[...]
        return k_ref[pl.ds(base + item[0], tk), :], vt_ref[:, pl.ds(base + item[0], tk)]

    def chunk_update(c, s_ref, vt, key_offset):
        cols = slice(c * cq, (c + 1) * cq)

        def load_scores():
            s = s_ref[...]
            if key_offset is None:
                return s
            key = key_offset + lax.broadcasted_iota(jnp.int32, s.shape, 0)
            query = (c * cq) % t + lax.broadcasted_iota(jnp.int32, s.shape, 1)
            return jnp.where(key <= query, s, MASKED)

        m_old = m_scr[:, cols]
        m_new = jnp.maximum(m_old, jnp.max(load_scores(), axis=0, keepdims=True))
        alpha = jnp.exp2(m_old - m_new)
        p = jnp.exp2(load_scores() - m_new)
        m_scr[:, cols] = m_new
        acc_scr[:, cols] = alpha * acc_scr[:, cols] + jnp.dot(vt, p.astype(BF16), preferred_element_type=F32)

    n_slots = s_scr.shape[0]
    ahead = n_slots // 2

    def issue_scores(base, item, slot):
        k = keys_of(base, item)[0]
        s_scr[slot, 0:k.shape[0], :] = scores(k, item[1])

    def run(base, items, next_items):
        assert len(items) % n_slots == 0 or not next_items
        upcoming = items + next_items
        for n, item in enumerate(items):
            if n + ahead < len(upcoming):
                issue_scores(base, upcoming[n + ahead], (n + ahead) % n_slots)
            vt = keys_of(base, item)[1]
            chunk_update(item[1], s_scr.at[n % n_slots, 0:vt.shape[1], :], vt, item[2])

    visible_items = [(d * tk, c, None) for d in range(t // tk) for c in range(n_chunks)]
    diagonal_items = []
    for d in range(t // tk):
        for c in range(n_chunks):
            first_query = (c * cq) % t
            if d * tk <= first_query + cq - 1:
                fully_visible = d * tk + tk - 1 <= first_query
                assert d * tk <= first_query
                diagonal_items.append((d * tk, c, None if fully_visible else d * tk))
    diagonal_items += [(None, c, None) for c in range(n_chunks)]

    m_scr[...] = jnp.full_like(m_scr, MASKED)
    acc_scr[...] = jnp.zeros_like(acc_scr)
    assert [item[:2] for item in diagonal_items[:ahead]] == [item[:2] for item in visible_items[:ahead]]
    for n in range(ahead):
        issue_scores(0, visible_items[n], n)

    def body(g, carry):
        run(pl.multiple_of(g * t, t), visible_items, [(t + rel, c, None) for rel, c, _ in visible_items[:ahead]])
        return carry

    lax.fori_loop(0, i, body, 0)
    run(pl.multiple_of(i * t, t), diagonal_items, [])

    lam = (jnp.exp(jnp.sum(lq1_ref[...] * lk1_ref[...], axis=-1, keepdims=True))
           - jnp.exp(jnp.sum(lq2_ref[...] * lk2_ref[...], axis=-1, keepdims=True)) + LAMBDA_INIT)
    o = acc_scr[0:V_DIM, :] / acc_scr[V_DIM:V_DIM + 1, :]
    out = o[:, 0:t] - lam * o[:, t:2 * t]
    normed = out * lax.rsqrt(jnp.mean(out * out, axis=0, keepdims=True) + RMS_EPS)
    o_ref[...] = ((normed * g_ref[...]) * (1.0 - LAMBDA_INIT)).T.astype(BF16)


def _attention(qk, vt, qk_meta, vt_meta, lq1, lk1, lq2, lk2, subln, *, t=2048, tk=512, cq=2 * MXU_WIDTH,
               cast=(), steps_per_cast=2):
    m = qk.shape[0]
    lam_vec = pl.BlockSpec((1, HEAD_DIM), lambda h, i: (0, 0))
    casts_per_head = m // t // steps_per_cast
    cast_in, cast_out, cast_shapes = _cast_plumbing(
        cast, N_HEADS * casts_per_head, lambda h, i: (h * casts_per_head + i // steps_per_cast, 0))
    return pl.pallas_call(
        functools.partial(_attn_kernel, t=t, tk=tk, cq=cq, n_cast=len(cast), steps_per_cast=steps_per_cast),
        grid=(N_HEADS, m // t),
        in_specs=[
            pl.BlockSpec((t, V_DIM), lambda h, i: (i, h)),
            pl.BlockSpec((m, V_DIM), lambda h, i: (0, N_HEADS + h)),
            pl.BlockSpec((VT_ROWS, m), lambda h, i: (h, 0)),
            pl.BlockSpec((N_META, V_DIM), lambda h, i: (0, N_HEADS + h)),
            pl.BlockSpec((VT_ROWS, N_META), lambda h, i: (h, 0)),
            lam_vec, lam_vec, lam_vec, lam_vec,
            pl.BlockSpec((V_DIM, 1), lambda h, i: (0, 0)),
        ] + cast_in,
        out_specs=[pl.BlockSpec((t, V_DIM), lambda h, i: (i, h))] + cast_out,
        out_shape=[jax.ShapeDtypeStruct((m, ATT_WIDTH), BF16)] + cast_shapes,
        scratch_shapes=[pltpu.VMEM((2 * t, V_DIM), BF16), pltpu.VMEM((4, tk, cq), F32),
                        pltpu.VMEM((1, 2 * t), F32), pltpu.VMEM((VT_ROWS, 2 * t), F32)],
        compiler_params=_params("parallel", "arbitrary"),
        name="diff_attention",
    )(qk, qk, vt, qk_meta, vt_meta,
      lq1.reshape(1, HEAD_DIM), lk1.reshape(1, HEAD_DIM), lq2.reshape(1, HEAD_DIM), lk2.reshape(1, HEAD_DIM),
      subln.reshape(V_DIM, 1), *cast)


def _merge_kernel(h_ref, ya_ref, yc_ref, s_ref, wg0_ref, wg1_ref, wb0_ref, wb1_ref, wo_ref, o_ref):
    j = pl.program_id(1)

    def chunk_term():
        h = h_ref[...]
        gate_a = jax.nn.sigmoid(jnp.dot(h, wg0_ref[...], preferred_element_type=F32))
        gate_c = jax.nn.sigmoid(jnp.dot(h, wg1_ref[...], preferred_element_type=F32))
        z_a = jnp.dot(ya_ref[...], wb0_ref[...], preferred_element_type=F32)
        z_c = jnp.dot(yc_ref[...], wb1_ref[...], preferred_element_type=F32)
        merged = (gate_a * z_a + gate_c * z_c).astype(BF16)
        return jnp.dot(merged, wo_ref[...], preferred_element_type=F32)

    @pl.when(j == 0)
    def _():
        o_ref[...] = s_ref[...] + chunk_term()

    @pl.when(j > 0)
    def _():
        o_ref[...] += chunk_term()


def _merge(h, y_att, y_conv, s, w_gate, w_branch, w_out, *, tm=512, tn=512):
    m = h.shape[0]
    n_d = D_MODEL // tn
    return pl.pallas_call(
        _merge_kernel,
        grid=(m // tm, n_d),
        in_specs=[
            pl.BlockSpec((tm, D_MODEL), lambda i, j: (i, 0)),
            pl.BlockSpec((tm, ATT_WIDTH), lambda i, j: (i, 0)),
            pl.BlockSpec((tm, CONV_WIDTH), lambda i, j: (i, 0)),
            pl.BlockSpec((tm, D_MODEL), lambda i, j: (i, 0)),
            pl.BlockSpec((D_MODEL, tn), lambda i, j: (0, j)),
            pl.BlockSpec((D_MODEL, tn), lambda i, j: (0, n_d + j)),
            pl.BlockSpec((None, ATT_WIDTH, tn), lambda i, j: (0, 0, j)),
            pl.BlockSpec((None, CONV_WIDTH, tn), lambda i, j: (1, 0, j)),
            pl.BlockSpec((tn, D_MODEL), lambda i, j: (j, 0)),
        ],
        out_specs=pl.BlockSpec((tm, D_MODEL), lambda i, j: (i, 0)),
        out_shape=jax.ShapeDtypeStruct((m, D_MODEL), F32),
        compiler_params=_params("parallel", "arbitrary"),
        name="gated_merge",
    )(h, y_att, y_conv, s, w_gate, w_gate, w_branch, w_branch, w_out)


def kernel(x, meta_tokens, norm_ffn1, ffn1_w_gu, ffn1_w_down, norm_mix, w_in, conv_w, lambda_q1, lambda_k1,
           lambda_q2, lambda_k2, subln, w_branch, w_gate, w_out, norm_ffn2, ffn2_w_gu, ffn2_w_down, norm_final):
    bsz, seq, _ = x.shape
    assert bsz == 1 and meta_tokens.shape == (N_META, D_MODEL)
    xs = x.reshape(seq, D_MODEL)
    _, h_meta, w_gate1, w_up1, w_down1 = _ffn(
        meta_tokens.astype(x.dtype), norm_ffn1[0], ffn1_w_gu[0], ffn1_w_gu[0], ffn1_w_down[0], norm_mix[0],
        final=False, tm=N_META, emit_weights=True)
    s1, h, w_in_bf = _ffn(xs, norm_ffn1[0], w_gate1, w_up1, w_down1, norm_mix[0], final=False, tm=512,
                          cast=(w_in[0],))

    qk_meta, v_meta, _, meta_tail = _mixer_proj(h_meta, w_in_bf, conv_w[0], jnp.zeros((SUBLANES, CONV_WIDTH), F32),
                                                tm=N_META, transpose_v=False)
    qk, vt, y_conv, _ = _mixer_proj(h, w_in_bf, conv_w[0], meta_tail, tm=512, transpose_v=True)

    later_weights = (w_gate[0], w_branch[0].reshape(N_BRANCH * ATT_WIDTH, D_MODEL), w_out[0],
                     ffn2_w_gu[0], ffn2_w_down[0])
    vt_meta = jnp.concatenate(
        [v_meta.reshape(N_META, N_HEADS, V_DIM).transpose(1, 2, 0),
         jnp.zeros((N_HEADS, BF16_ROWS, N_META), BF16).at[:, 0, :].set(1)], axis=1).reshape(-1, N_META)
    y_att, w_gate_bf, w_branch_bf, w_out_bf, w_gu2, w_d2 = _attention(
        qk, vt, qk_meta, vt_meta, lambda_q1[0], lambda_k1[0], lambda_q2[0], lambda_k2[0], subln[0],
        cast=later_weights)
    s2 = _merge(h, y_att, y_conv, s1, w_gate_bf, w_branch_bf.reshape(N_BRANCH, ATT_WIDTH, D_MODEL), w_out_bf)
    out, = _ffn(s2, norm_ffn2[0], w_gu2, w_gu2, w_d2, norm_final, final=True, tm=1024)
    return out.reshape(bsz, seq, D_MODEL)
```

```python
import functools
import math

import jax
import jax.numpy as jnp
from jax import lax
from jax.experimental import pallas as pl
from jax.experimental.pallas import tpu as pltpu

D_MODEL = 2048
N_META = 16
HEAD_DIM = 64
V_DIM = 2 * HEAD_DIM
ATT_WIDTH = D_MODEL // 2
N_HEADS = ATT_WIDTH // V_DIM
CONV_WIDTH = D_MODEL // 2
CONV_K = 3
N_BRANCH = 2
D_FF = ((8 * D_MODEL // 3 + 255) // 256) * 256
RMS_EPS = 1e-6
LAMBDA_INIT = 0.8 - 0.6 * math.exp(-0.3 * 0)
MASKED = -1e30
Q_SCALE = math.log2(math.e) * HEAD_DIM ** -0.5

SUBLANES = 8
BF16_ROWS = 16
VT_ROWS = V_DIM + BF16_ROWS
MXU_WIDTH = 256
VMEM_LIMIT_BYTES = 56 * 1024 * 1024

BF16 = jnp.bfloat16
F32 = jnp.float32
NT_DIMS = (((1,), (1,)), ((), ()))


def _rms_scale(x):
    return x * lax.rsqrt(jnp.mean(x * x, axis=-1, keepdims=True) + RMS_EPS)


def _params(*semantics):
    return pltpu.CompilerParams(dimension_semantics=semantics, vmem_limit_bytes=VMEM_LIMIT_BYTES)


def _cast_plumbing(weights, n_blocks, index_map):
    specs, shapes = [], []
    for w in weights:
        rows, cols = w.shape
        assert rows % (n_blocks * BF16_ROWS) == 0, (w.shape, n_blocks)
        specs.append(pl.BlockSpec((rows // n_blocks, cols), index_map))
        shapes.append(jax.ShapeDtypeStruct(w.shape, BF16))
    return specs, list(specs), shapes


def _cast_blocks(src_refs, dst_refs):
    for src, dst in zip(src_refs, dst_refs):
        dst[...] = src[...].astype(BF16)


def _ffn_kernel(x_ref, g_ref, wg_ref, wu_ref, wd_ref, *refs, final, n_cast, emit_weights):
    if final:
        gn_ref, refs = refs[0], refs[1:]
    cast_in, acc_ref, cast_out, refs = refs[:n_cast], refs[n_cast], refs[n_cast + 1:2 * n_cast + 1], refs[2 * n_cast + 1:]
    own_out, (h_scr,) = (refs[:3], refs[3:]) if emit_weights else ((), refs)
    j, last = pl.program_id(1), pl.num_programs(1) - 1
    tm = h_scr.shape[0]
    halves = [slice(0, tm // 2), slice(tm // 2, tm)] if tm % (2 * MXU_WIDTH) == 0 else [slice(0, tm)]

    def weights():
        _cast_blocks(cast_in, cast_out)
        ws = [ref[...].astype(BF16) for ref in (wg_ref, wu_ref, wd_ref)]
        for dst, w in zip(own_out, ws):
            dst[...] = w
        return ws

    def gate_up(h, w_gate, w_up):
        return jnp.dot(h, w_gate, preferred_element_type=F32), jnp.dot(h, w_up, preferred_element_type=F32)

    def down(gate, up, w_down):
        act = (gate * jax.nn.sigmoid(gate) * up).astype(BF16)
        return jnp.dot(act, w_down, preferred_element_type=F32)

    @pl.when(j == 0)
    def _():
        w_gate, w_up, w_down = weights()
        gate_ups = []
        for rows in halves:
            h = (_rms_scale(x_ref[rows, :]) * g_ref[...]).astype(BF16)
            h_scr[rows, :] = h
            gate_ups.append(gate_up(h, w_gate, w_up))
        for rows, (gate, up) in zip(halves, gate_ups):
            acc_ref[rows, :] = down(gate, up, w_down)

    @pl.when(jnp.logical_and(j > 0, j < last))
    def _():
        w_gate, w_up, w_down = weights()
        acc_ref[...] += down(*gate_up(h_scr[...], w_gate, w_up), w_down)

    @pl.when(j == last)
    def _():
        w_gate, w_up, w_down = weights()
        gate_ups = [gate_up(h_scr[rows, :], w_gate, w_up) for rows in halves]
        for rows, (gate, up) in zip(halves, gate_ups):
            s = x_ref[rows, :] + 0.5 * (acc_ref[rows, :] + down(gate, up, w_down))
            acc_ref[rows, :] = _rms_scale(s) * gn_ref[...] if final else s


def _ffn(x, g, w_gate, w_up, w_down, *, tm, norm_out=None, tf=512, cast=(), cast_steps=8, emit_weights=False):
    m = x.shape[0]
    n_f = D_FF // tf
    assert n_f >= max(2, cast_steps)
    final = norm_out is not None
    up_block = w_up.shape[1] // tf - n_f
    row = pl.BlockSpec((tm, D_MODEL), lambda i, j: (i, 0))
    vec = pl.BlockSpec((1, D_MODEL), lambda i, j: (0, 0))
    gate_or_up = pl.BlockSpec((D_MODEL, tf), lambda i, j: (0, j))
    down = pl.BlockSpec((tf, D_MODEL), lambda i, j: (j, 0))
    cast_in, cast_out, cast_shapes = _cast_plumbing(
        cast, m // tm * cast_steps, lambda i, j: (i * cast_steps + jnp.minimum(j, cast_steps - 1), 0))
    if emit_weights:
        assert m == tm
        cast_out = cast_out + [gate_or_up, gate_or_up, down]
        cast_shapes = cast_shapes + [jax.ShapeDtypeStruct(s, BF16) for s in
                                     ((D_MODEL, D_FF), (D_MODEL, D_FF), (D_FF, D_MODEL))]
    return pl.pallas_call(
        functools.partial(_ffn_kernel, final=final, n_cast=len(cast), emit_weights=emit_weights),
        grid=(m // tm, n_f),
        in_specs=[row, vec, gate_or_up, pl.BlockSpec((D_MODEL, tf), lambda i, j: (0, up_block + j)), down]
                 + [vec] * final + cast_in,
        out_specs=[row] + cast_out,
        out_shape=[jax.ShapeDtypeStruct((m, D_MODEL), F32)] + cast_shapes,
        scratch_shapes=[pltpu.VMEM((tm, D_MODEL), BF16)],
        compiler_params=_params("parallel", "arbitrary"),
        name="ffn_final" if final else "ffn_mix",
    )(x, g.reshape(1, D_MODEL), w_gate, w_up, w_down, *([norm_out.reshape(1, D_MODEL)] if final else []), *cast)


def _mixer_proj_kernel(s_ref, g_ref, w_ref, cw_ref, cin_ref, h_ref, qk_ref, v_ref, y_ref, cout_ref, carry_scr, *,
                       transpose_v):
    i = pl.program_id(0)
    tm = s_ref.shape[0]

    @pl.when(i == 0)
    def _():
        carry_scr[...] = cin_ref[...]

    h = (_rms_scale(s_ref[...]) * g_ref[...]).astype(BF16)
    h_ref[...] = h

    def proj(first_col, width):
        return jnp.dot(h, w_ref[:, first_col:first_col + width], preferred_element_type=F32)

    qk_ref[:, 0:ATT_WIDTH] = (proj(0, ATT_WIDTH) * Q_SCALE).astype(BF16)
    qk_ref[:, ATT_WIDTH:2 * ATT_WIDTH] = proj(ATT_WIDTH, ATT_WIDTH).astype(BF16)
    v_proj = proj(2 * ATT_WIDTH, ATT_WIDTH)
    if transpose_v:
        v_t = v_proj.T
        ones_then_zeros = (lax.broadcasted_iota(jnp.int32, (BF16_ROWS, tm), 0) == 0).astype(BF16)
        for head in range(N_HEADS):
            v_ref[head * VT_ROWS:head * VT_ROWS + V_DIM, :] = v_t[head * V_DIM:(head + 1) * V_DIM, :].astype(BF16)
            v_ref[head * VT_ROWS + V_DIM:(head + 1) * VT_ROWS, :] = ones_then_zeros
    else:
        v_ref[...] = v_proj.astype(BF16)

    c0 = 3 * ATT_WIDTH
    b_gate = proj(c0, CONV_WIDTH)
    v = proj(c0 + CONV_WIDTH, CONV_WIDTH) * proj(c0 + 2 * CONV_WIDTH, CONV_WIDTH)
    w0, w1, w2 = cw_ref[0:1, :], cw_ref[1:2, :], cw_ref[2:3, :]

    y = w0 * pltpu.roll(v, 2, axis=0) + w1 * pltpu.roll(v, 1, axis=0) + w2 * v
    y_ref[...] = (b_gate * y).astype(BF16)

    prev = carry_scr[...]
    head = v[0:SUBLANES, :]
    rows = lax.broadcasted_iota(jnp.int32, head.shape, 0)
    v1 = jnp.where(rows >= 1, pltpu.roll(head, 1, axis=0), pltpu.roll(prev, 1, axis=0))
    v2 = jnp.where(rows >= 2, pltpu.roll(head, 2, axis=0), pltpu.roll(prev, 2, axis=0))
    y_head = w0 * v2 + w1 * v1 + w2 * head
    y_ref[0:SUBLANES, :] = (b_gate[0:SUBLANES, :] * y_head).astype(BF16)

    tail = v[tm - SUBLANES:tm, :]
    carry_scr[...] = tail
    cout_ref[...] = tail


def _mixer_proj(s, g, w_in, conv_w, carry_in, *, tm, transpose_v):
    m = s.shape[0]
    whole = functools.partial(pl.BlockSpec, index_map=lambda i: (0, 0), pipeline_mode=pl.Buffered(1))
    row = pl.BlockSpec((tm, D_MODEL), lambda i: (i, 0))
    if transpose_v:
        v_spec, v_shape = pl.BlockSpec((N_HEADS * VT_ROWS, tm), lambda i: (0, i)), (N_HEADS * VT_ROWS, m)
    else:
        v_spec, v_shape = pl.BlockSpec((tm, ATT_WIDTH), lambda i: (i, 0)), (m, ATT_WIDTH)
    return pl.pallas_call(
        functools.partial(_mixer_proj_kernel, transpose_v=transpose_v),
        grid=(m // tm,),
        in_specs=[row, whole((1, D_MODEL)), whole(w_in.shape), whole((CONV_K, CONV_WIDTH)),
                  whole((SUBLANES, CONV_WIDTH))],
        out_specs=(row,
                   pl.BlockSpec((tm, 2 * ATT_WIDTH), lambda i: (i, 0)),
                   v_spec,
                   pl.BlockSpec((tm, CONV_WIDTH), lambda i: (i, 0)),
                   pl.BlockSpec((SUBLANES, CONV_WIDTH), lambda i: (i, 0))),
        out_shape=(jax.ShapeDtypeStruct((m, D_MODEL), BF16),
                   jax.ShapeDtypeStruct((m, 2 * ATT_WIDTH), BF16),
                   jax.ShapeDtypeStruct(v_shape, BF16),
                   jax.ShapeDtypeStruct((m, CONV_WIDTH), BF16),
                   jax.ShapeDtypeStruct((m // tm * SUBLANES, CONV_WIDTH), F32)),
        scratch_shapes=[pltpu.VMEM((SUBLANES, CONV_WIDTH), F32)],
        compiler_params=_params("arbitrary"),
        name="mixer_proj",
    )(s, g.reshape(1, D_MODEL), w_in, conv_w, carry_in)


def _attn_kernel(q_ref, k_ref, vt_ref, km_ref, vtm_ref, lq1_ref, lk1_ref, lq2_ref, lk2_ref, g_ref, *refs,
                 t, tk, cq, n_cast, steps_per_cast):
    cast_in, o_ref, cast_out = refs[:n_cast], refs[n_cast], refs[n_cast + 1:2 * n_cast + 1]
    q2_scr, s_scr, m_scr, acc_scr = refs[2 * n_cast + 1:]
    i = pl.program_id(1)
    n_chunks = 2 * t // cq

    @pl.when(i % steps_per_cast == 0)
    def _():
        _cast_blocks(cast_in, cast_out)

    q = q_ref[...]
    lane = lax.broadcasted_iota(jnp.int32, q.shape, 1)
    zero = jnp.zeros_like(q)
    q2_scr[0:t, :] = jnp.where(lane < HEAD_DIM, q, zero)
    q2_scr[t:2 * t, :] = jnp.where(lane >= HEAD_DIM, q, zero)

    def scores(k, c):
        return lax.dot_general(k, q2_scr[c * cq:(c + 1) * cq, :], NT_DIMS, preferred_element_type=F32)

    def keys_of(base, item):
        if item[0] is None:
            return km_ref[...], vtm_ref[...]
        return k_ref[pl.ds(base + item[0], tk), :], vt_ref[:, pl.ds(base + item[0], tk)]

    def chunk_update(c, s_ref, vt, key_offset):
        cols = slice(c * cq, (c + 1) * cq)

        def load_scores():
            s = s_ref[...]
            if key_offset is None:
                return s
            key = key_offset + lax.broadcasted_iota(jnp.int32, s.shape, 0)
            query = (c * cq) % t + lax.broadcasted_iota(jnp.int32, s.shape, 1)
            return jnp.where(key <= query, s, MASKED)

        m_old = m_scr[:, cols]
        m_new = jnp.maximum(m_old, jnp.max(load_scores(), axis=0, keepdims=True))
        alpha = jnp.exp2(m_old - m_new)
        p = jnp.exp2(load_scores() - m_new)
        m_scr[:, cols] = m_new
        acc_scr[:, cols] = alpha * acc_scr[:, cols] + jnp.dot(vt, p.astype(BF16), preferred_element_type=F32)

    n_slots = s_scr.shape[0]
    ahead = n_slots // 2

    def issue_scores(base, item, slot):
        k = keys_of(base, item)[0]
        s_scr[slot, 0:k.shape[0], :] = scores(k, item[1])

    def run(base, items, next_items):
        assert len(items) % n_slots == 0 or not next_items
        upcoming = items + next_items
        for n, item in enumerate(items):
            if n + ahead < len(upcoming):
                issue_scores(base, upcoming[n + ahead], (n + ahead) % n_slots)
            vt = keys_of(base, item)[1]
            chunk_update(item[1], s_scr.at[n % n_slots, 0:vt.shape[1], :], vt, item[2])

    visible_items = [(d * tk, c, None) for d in range(t // tk) for c in range(n_chunks)]
    diagonal_items = []
    for d in range(t // tk):
        for c in range(n_chunks):
            first_query = (c * cq) % t
            if d * tk <= first_query + cq - 1:
                fully_visible = d * tk + tk - 1 <= first_query
                assert d * tk <= first_query
                diagonal_items.append((d * tk, c, None if fully_visible else d * tk))
    diagonal_items += [(None, c, None) for c in range(n_chunks)]

    m_scr[...] = jnp.full_like(m_scr, MASKED)
    acc_scr[...] = jnp.zeros_like(acc_scr)
    assert [item[:2] for item in diagonal_items[:ahead]] == [item[:2] for item in visible_items[:ahead]]
    for n in range(ahead):
        issue_scores(0, visible_items[n], n)

    def body(g, carry):
        run(pl.multiple_of(g * t, t), visible_items, [(t + rel, c, None) for rel, c, _ in visible_items[:ahead]])
        return carry

    lax.fori_loop(0, i, body, 0)
    run(pl.multiple_of(i * t, t), diagonal_items, [])

    lam = (jnp.exp(jnp.sum(lq1_ref[...] * lk1_ref[...], axis=-1, keepdims=True))
           - jnp.exp(jnp.sum(lq2_ref[...] * lk2_ref[...], axis=-1, keepdims=True)) + LAMBDA_INIT)
    o = acc_scr[0:V_DIM, :] / acc_scr[V_DIM:V_DIM + 1, :]
    out = o[:, 0:t] - lam * o[:, t:2 * t]
    normed = out * lax.rsqrt(jnp.mean(out * out, axis=0, keepdims=True) + RMS_EPS)
    o_ref[...] = ((normed * g_ref[...]) * (1.0 - LAMBDA_INIT)).T.astype(BF16)


def _attention(qk, vt, qk_meta, vt_meta, lq1, lk1, lq2, lk2, subln, *, t=2048, tk=512, cq=2 * MXU_WIDTH,
               cast=(), steps_per_cast=2):
    m = qk.shape[0]
    lam_vec = pl.BlockSpec((1, HEAD_DIM), lambda h, i: (0, 0))
    casts_per_head = m // t // steps_per_cast
    cast_in, cast_out, cast_shapes = _cast_plumbing(
        cast, N_HEADS * casts_per_head, lambda h, i: (h * casts_per_head + i // steps_per_cast, 0))
    return pl.pallas_call(
        functools.partial(_attn_kernel, t=t, tk=tk, cq=cq, n_cast=len(cast), steps_per_cast=steps_per_cast),
        grid=(N_HEADS, m // t),
        in_specs=[
            pl.BlockSpec((t, V_DIM), lambda h, i: (i, h)),
            pl.BlockSpec((m, V_DIM), lambda h, i: (0, N_HEADS + h)),
            pl.BlockSpec((VT_ROWS, m), lambda h, i: (h, 0)),
            pl.BlockSpec((N_META, V_DIM), lambda h, i: (0, N_HEADS + h)),
            pl.BlockSpec((VT_ROWS, N_META), lambda h, i: (h, 0)),
            lam_vec, lam_vec, lam_vec, lam_vec,
            pl.BlockSpec((V_DIM, 1), lambda h, i: (0, 0)),
        ] + cast_in,
        out_specs=[pl.BlockSpec((t, V_DIM), lambda h, i: (i, h))] + cast_out,
        out_shape=[jax.ShapeDtypeStruct((m, ATT_WIDTH), BF16)] + cast_shapes,
        scratch_shapes=[pltpu.VMEM((2 * t, V_DIM), BF16), pltpu.VMEM((4, tk, cq), F32),
                        pltpu.VMEM((1, 2 * t), F32), pltpu.VMEM((VT_ROWS, 2 * t), F32)],
        compiler_params=_params("parallel", "arbitrary"),
        name="diff_attention",
    )(qk, qk, vt, qk_meta, vt_meta,
      lq1.reshape(1, HEAD_DIM), lk1.reshape(1, HEAD_DIM), lq2.reshape(1, HEAD_DIM), lk2.reshape(1, HEAD_DIM),
      subln.reshape(V_DIM, 1), *cast)


def _merge_kernel(h_ref, ya_ref, yc_ref, s_ref, wg0_ref, wg1_ref, wb0_ref, wb1_ref, wo_ref, o_ref):
    j = pl.program_id(1)

    def chunk_term():
        h = h_ref[...]
        gate_a = jax.nn.sigmoid(jnp.dot(h, wg0_ref[...], preferred_element_type=F32))
        gate_c = jax.nn.sigmoid(jnp.dot(h, wg1_ref[...], preferred_element_type=F32))
        z_a = jnp.dot(ya_ref[...], wb0_ref[...], preferred_element_type=F32)
        z_c = jnp.dot(yc_ref[...], wb1_ref[...], preferred_element_type=F32)
        merged = (gate_a * z_a + gate_c * z_c).astype(BF16)
        return jnp.dot(merged, wo_ref[...], preferred_element_type=F32)

    @pl.when(j == 0)
    def _():
        o_ref[...] = s_ref[...] + chunk_term()

    @pl.when(j > 0)
    def _():
        o_ref[...] += chunk_term()


def _merge(h, y_att, y_conv, s, w_gate, w_branch, w_out, *, tm=512, tn=512):
    m = h.shape[0]
    n_d = D_MODEL // tn
    return pl.pallas_call(
        _merge_kernel,
        grid=(m // tm, n_d),
        in_specs=[
            pl.BlockSpec((tm, D_MODEL), lambda i, j: (i, 0)),
            pl.BlockSpec((tm, ATT_WIDTH), lambda i, j: (i, 0)),
            pl.BlockSpec((tm, CONV_WIDTH), lambda i, j: (i, 0)),
            pl.BlockSpec((tm, D_MODEL), lambda i, j: (i, 0)),
            pl.BlockSpec((D_MODEL, tn), lambda i, j: (0, j)),
            pl.BlockSpec((D_MODEL, tn), lambda i, j: (0, n_d + j)),
            pl.BlockSpec((None, ATT_WIDTH, tn), lambda i, j: (0, 0, j)),
            pl.BlockSpec((None, CONV_WIDTH, tn), lambda i, j: (1, 0, j)),
            pl.BlockSpec((tn, D_MODEL), lambda i, j: (j, 0)),
        ],
        out_specs=pl.BlockSpec((tm, D_MODEL), lambda i, j: (i, 0)),
        out_shape=jax.ShapeDtypeStruct((m, D_MODEL), F32),
        compiler_params=_params("parallel", "arbitrary"),
        name="gated_merge",
    )(h, y_att, y_conv, s, w_gate, w_gate, w_branch, w_branch, w_out)


def kernel(x, meta_tokens, norm_ffn1, ffn1_w_gu, ffn1_w_down, norm_mix, w_in, conv_w, lambda_q1, lambda_k1,
           lambda_q2, lambda_k2, subln, w_branch, w_gate, w_out, norm_ffn2, ffn2_w_gu, ffn2_w_down, norm_final):
    bsz, seq, _ = x.shape
    assert bsz == 1 and meta_tokens.shape == (N_META, D_MODEL)
    xs = x.reshape(seq, D_MODEL)
    s1_meta, w_gate1, w_up1, w_down1 = _ffn(meta_tokens.astype(x.dtype), norm_ffn1[0], ffn1_w_gu[0], ffn1_w_gu[0],
                                            ffn1_w_down[0], tm=N_META, emit_weights=True)
    s1, w_in_bf = _ffn(xs, norm_ffn1[0], w_gate1, w_up1, w_down1, tm=1024, cast=(w_in[0],))

    _, qk_meta, v_meta, _, meta_tail = _mixer_proj(
        s1_meta, norm_mix[0], w_in_bf, conv_w[0], jnp.zeros((SUBLANES, CONV_WIDTH), F32), tm=N_META, transpose_v=False)
    h, qk, vt, y_conv, _ = _mixer_proj(s1, norm_mix[0], w_in_bf, conv_w[0], meta_tail, tm=512, transpose_v=True)

    later_weights = (w_gate[0], w_branch[0].reshape(N_BRANCH * ATT_WIDTH, D_MODEL), w_out[0],
                     ffn2_w_gu[0], ffn2_w_down[0])
    vt_meta = jnp.concatenate(
        [v_meta.reshape(N_META, N_HEADS, V_DIM).transpose(1, 2, 0),
         jnp.zeros((N_HEADS, BF16_ROWS, N_META), BF16).at[:, 0, :].set(1)], axis=1).reshape(-1, N_META)
    y_att, w_gate_bf, w_branch_bf, w_out_bf, w_gu2, w_d2 = _attention(
        qk, vt, qk_meta, vt_meta, lambda_q1[0], lambda_k1[0], lambda_q2[0], lambda_k2[0], subln[0],
        cast=later_weights)
    s2 = _merge(h, y_att, y_conv, s1, w_gate_bf, w_branch_bf.reshape(N_BRANCH, ATT_WIDTH, D_MODEL), w_out_bf)
    out, = _ffn(s2, norm_ffn2[0], w_gu2, w_gu2, w_d2, norm_out=norm_final, tm=1024)
    return out.reshape(bsz, seq, D_MODEL)
```

```python
import functools
import math

import jax
import jax.numpy as jnp
from jax import lax
from jax.experimental import pallas as pl
from jax.experimental.pallas import tpu as pltpu

D_MODEL = 2048
N_META = 16
HEAD_DIM = 64
V_DIM = 2 * HEAD_DIM
ATT_WIDTH = D_MODEL // 2
N_HEADS = ATT_WIDTH // V_DIM
CONV_WIDTH = D_MODEL // 2
CONV_K = 3
N_BRANCH = 2
D_FF = ((8 * D_MODEL // 3 + 255) // 256) * 256
RMS_EPS = 1e-6
LAMBDA_INIT = 0.8 - 0.6 * math.exp(-0.3 * 0)
MASKED = -1e30
Q_SCALE = math.log2(math.e) * HEAD_DIM ** -0.5

SUBLANES = 8
BF16_ROWS = 16
VT_ROWS = V_DIM + BF16_ROWS
MXU_WIDTH = 256
VMEM_LIMIT_BYTES = 56 * 1024 * 1024

BF16 = jnp.bfloat16
F32 = jnp.float32
NT_DIMS = (((1,), (1,)), ((), ()))


def _rms_scale(x):
    return x * lax.rsqrt(jnp.mean(x * x, axis=-1, keepdims=True) + RMS_EPS)


def _params(*semantics):
    return pltpu.CompilerParams(dimension_semantics=semantics, vmem_limit_bytes=VMEM_LIMIT_BYTES)


def _cast_plumbing(weights, n_blocks, index_map):
    specs, shapes = [], []
    for w in weights:
        rows, cols = w.shape
        assert rows % (n_blocks * BF16_ROWS) == 0, (w.shape, n_blocks)
        specs.append(pl.BlockSpec((rows // n_blocks, cols), index_map))
        shapes.append(jax.ShapeDtypeStruct(w.shape, BF16))
    return specs, list(specs), shapes


def _cast_blocks(src_refs, dst_refs):
    for src, dst in zip(src_refs, dst_refs):
        dst[...] = src[...].astype(BF16)


def _ffn_kernel(x_ref, g_ref, wg_ref, wu_ref, wd_ref, *refs, final, n_cast, emit_weights):
    if final:
        gn_ref, refs = refs[0], refs[1:]
    cast_in, acc_ref, cast_out, refs = refs[:n_cast], refs[n_cast], refs[n_cast + 1:2 * n_cast + 1], refs[2 * n_cast + 1:]
    own_out, (h_scr,) = (refs[:3], refs[3:]) if emit_weights else ((), refs)
    j, last = pl.program_id(1), pl.num_programs(1) - 1
    tm = h_scr.shape[0]
    halves = [slice(0, tm // 2), slice(tm // 2, tm)] if tm % (2 * MXU_WIDTH) == 0 else [slice(0, tm)]

    def weights():
        _cast_blocks(cast_in, cast_out)
        ws = [ref[...].astype(BF16) for ref in (wg_ref, wu_ref, wd_ref)]
        for dst, w in zip(own_out, ws):
            dst[...] = w
        return ws

    def gate_up(h, w_gate, w_up):
        return jnp.dot(h, w_gate, preferred_element_type=F32), jnp.dot(h, w_up, preferred_element_type=F32)

    def down(gate, up, w_down):
        act = (gate * jax.nn.sigmoid(gate) * up).astype(BF16)
        return jnp.dot(act, w_down, preferred_element_type=F32)

    @pl.when(j == 0)
    def _():
        w_gate, w_up, w_down = weights()
        gate_ups = []
        for rows in halves:
            h = (_rms_scale(x_ref[rows, :]) * g_ref[...]).astype(BF16)
            h_scr[rows, :] = h
            gate_ups.append(gate_up(h, w_gate, w_up))
        for rows, (gate, up) in zip(halves, gate_ups):
            acc_ref[rows, :] = down(gate, up, w_down)

    @pl.when(jnp.logical_and(j > 0, j < last))
    def _():
        w_gate, w_up, w_down = weights()
        acc_ref[...] += down(*gate_up(h_scr[...], w_gate, w_up), w_down)

    @pl.when(j == last)
    def _():
        w_gate, w_up, w_down = weights()
        gate_ups = [gate_up(h_scr[rows, :], w_gate, w_up) for rows in halves]
        for rows, (gate, up) in zip(halves, gate_ups):
            s = x_ref[rows, :] + 0.5 * (acc_ref[rows, :] + down(gate, up, w_down))
            acc_ref[rows, :] = _rms_scale(s) * gn_ref[...] if final else s


def _ffn(x, g, w_gate, w_up, w_down, *, tm, norm_out=None, tf=512, cast=(), cast_steps=8, emit_weights=False):
    m = x.shape[0]
    n_f = D_FF // tf
    assert n_f >= max(2, cast_steps)
    final = norm_out is not None
    up_block = w_up.shape[1] // tf - n_f
    row = pl.BlockSpec((tm, D_MODEL), lambda i, j: (i, 0))
    vec = pl.BlockSpec((1, D_MODEL), lambda i, j: (0, 0))
    gate_or_up = pl.BlockSpec((D_MODEL, tf), lambda i, j: (0, j))
    down = pl.BlockSpec((tf, D_MODEL), lambda i, j: (j, 0))
    cast_in, cast_out, cast_shapes = _cast_plumbing(
        cast, m // tm * cast_steps, lambda i, j: (i * cast_steps + jnp.minimum(j, cast_steps - 1), 0))
    if emit_weights:
        assert m == tm
        cast_out = cast_out + [gate_or_up, gate_or_up, down]
        cast_shapes = cast_shapes + [jax.ShapeDtypeStruct(s, BF16) for s in
                                     ((D_MODEL, D_FF), (D_MODEL, D_FF), (D_FF, D_MODEL))]
    return pl.pallas_call(
        functools.partial(_ffn_kernel, final=final, n_cast=len(cast), emit_weights=emit_weights),
        grid=(m // tm, n_f),
        in_specs=[row, vec, gate_or_up, pl.BlockSpec((D_MODEL, tf), lambda i, j: (0, up_block + j)), down]
                 + [vec] * final + cast_in,
        out_specs=[row] + cast_out,
        out_shape=[jax.ShapeDtypeStruct((m, D_MODEL), F32)] + cast_shapes,
        scratch_shapes=[pltpu.VMEM((tm, D_MODEL), BF16)],
        compiler_params=_params("parallel", "arbitrary"),
        name="ffn_final" if final else "ffn_mix",
    )(x, g.reshape(1, D_MODEL), w_gate, w_up, w_down, *([norm_out.reshape(1, D_MODEL)] if final else []), *cast)


def _mixer_proj_kernel(s_ref, g_ref, w_ref, cw_ref, cin_ref, h_ref, qk_ref, v_ref, y_ref, cout_ref, carry_scr, *,
                       transpose_v):
    i = pl.program_id(0)
    tm = s_ref.shape[0]

    @pl.when(i == 0)
    def _():
        carry_scr[...] = cin_ref[...]

    h = (_rms_scale(s_ref[...]) * g_ref[...]).astype(BF16)
    h_ref[...] = h

    def proj(first_col, width):
        return jnp.dot(h, w_ref[:, first_col:first_col + width], preferred_element_type=F32)

    qk_ref[:, 0:ATT_WIDTH] = (proj(0, ATT_WIDTH) * Q_SCALE).astype(BF16)
    qk_ref[:, ATT_WIDTH:2 * ATT_WIDTH] = proj(ATT_WIDTH, ATT_WIDTH).astype(BF16)
    v_proj = proj(2 * ATT_WIDTH, ATT_WIDTH)
    if transpose_v:
        v_t = v_proj.T
        ones_then_zeros = (lax.broadcasted_iota(jnp.int32, (BF16_ROWS, tm), 0) == 0).astype(BF16)
        for head in range(N_HEADS):
            v_ref[head * VT_ROWS:head * VT_ROWS + V_DIM, :] = v_t[head * V_DIM:(head + 1) * V_DIM, :].astype(BF16)
            v_ref[head * VT_ROWS + V_DIM:(head + 1) * VT_ROWS, :] = ones_then_zeros
    else:
        v_ref[...] = v_proj.astype(BF16)

    c0 = 3 * ATT_WIDTH
    b_gate = proj(c0, CONV_WIDTH)
    v = proj(c0 + CONV_WIDTH, CONV_WIDTH) * proj(c0 + 2 * CONV_WIDTH, CONV_WIDTH)
    w0, w1, w2 = cw_ref[0:1, :], cw_ref[1:2, :], cw_ref[2:3, :]

    y = w0 * pltpu.roll(v, 2, axis=0) + w1 * pltpu.roll(v, 1, axis=0) + w2 * v
    y_ref[...] = (b_gate * y).astype(BF16)

    prev = carry_scr[...]
    head = v[0:SUBLANES, :]
    rows = lax.broadcasted_iota(jnp.int32, head.shape, 0)
    v1 = jnp.where(rows >= 1, pltpu.roll(head, 1, axis=0), pltpu.roll(prev, 1, axis=0))
    v2 = jnp.where(rows >= 2, pltpu.roll(head, 2, axis=0), pltpu.roll(prev, 2, axis=0))
    y_head = w0 * v2 + w1 * v1 + w2 * head
    y_ref[0:SUBLANES, :] = (b_gate[0:SUBLANES, :] * y_head).astype(BF16)

    tail = v[tm - SUBLANES:tm, :]
    carry_scr[...] = tail
    cout_ref[...] = tail


def _mixer_proj(s, g, w_in, conv_w, carry_in, *, tm, transpose_v):
    m = s.shape[0]
    whole = functools.partial(pl.BlockSpec, index_map=lambda i: (0, 0), pipeline_mode=pl.Buffered(1))
    row = pl.BlockSpec((tm, D_MODEL), lambda i: (i, 0))
    if transpose_v:
        v_spec, v_shape = pl.BlockSpec((N_HEADS * VT_ROWS, tm), lambda i: (0, i)), (N_HEADS * VT_ROWS, m)
    else:
        v_spec, v_shape = pl.BlockSpec((tm, ATT_WIDTH), lambda i: (i, 0)), (m, ATT_WIDTH)
    return pl.pallas_call(
        functools.partial(_mixer_proj_kernel, transpose_v=transpose_v),
        grid=(m // tm,),
        in_specs=[row, whole((1, D_MODEL)), whole(w_in.shape), whole((CONV_K, CONV_WIDTH)),
                  whole((SUBLANES, CONV_WIDTH))],
        out_specs=(row,
                   pl.BlockSpec((tm, 2 * ATT_WIDTH), lambda i: (i, 0)),
                   v_spec,
                   pl.BlockSpec((tm, CONV_WIDTH), lambda i: (i, 0)),
                   pl.BlockSpec((SUBLANES, CONV_WIDTH), lambda i: (i, 0))),
        out_shape=(jax.ShapeDtypeStruct((m, D_MODEL), BF16),
                   jax.ShapeDtypeStruct((m, 2 * ATT_WIDTH), BF16),
                   jax.ShapeDtypeStruct(v_shape, BF16),
                   jax.ShapeDtypeStruct((m, CONV_WIDTH), BF16),
                   jax.ShapeDtypeStruct((m // tm * SUBLANES, CONV_WIDTH), F32)),
        scratch_shapes=[pltpu.VMEM((SUBLANES, CONV_WIDTH), F32)],
        compiler_params=_params("arbitrary"),
        name="mixer_proj",
    )(s, g.reshape(1, D_MODEL), w_in, conv_w, carry_in)


def _attn_kernel(q_ref, k_ref, vt_ref, km_ref, vtm_ref, lq1_ref, lk1_ref, lq2_ref, lk2_ref, g_ref, *refs,
                 t, tk, cq, n_cast, steps_per_cast):
    cast_in, o_ref, cast_out = refs[:n_cast], refs[n_cast], refs[n_cast + 1:2 * n_cast + 1]
    q2_scr, s_scr, m_scr, acc_scr = refs[2 * n_cast + 1:]
    i = pl.program_id(1)
    n_chunks = 2 * t // cq

    @pl.when(i % steps_per_cast == 0)
    def _():
        _cast_blocks(cast_in, cast_out)

    q = q_ref[...]
    lane = lax.broadcasted_iota(jnp.int32, q.shape, 1)
    zero = jnp.zeros_like(q)
    q2_scr[0:t, :] = jnp.where(lane < HEAD_DIM, q, zero)
    q2_scr[t:2 * t, :] = jnp.where(lane >= HEAD_DIM, q, zero)

    def scores(k, c):
        return lax.dot_general(k, q2_scr[c * cq:(c + 1) * cq, :], NT_DIMS, preferred_element_type=F32)

    def keys_of(base, item):
        if item[0] is None:
            return km_ref[...], vtm_ref[...]
        return k_ref[pl.ds(base + item[0], tk), :], vt_ref[:, pl.ds(base + item[0], tk)]

    def chunk_update(c, s_ref, vt, key_offset):
        cols = slice(c * cq, (c + 1) * cq)

        def load_scores():
            s = s_ref[...]
            if key_offset is None:
                return s
            key = key_offset + lax.broadcasted_iota(jnp.int32, s.shape, 0)
            query = (c * cq) % t + lax.broadcasted_iota(jnp.int32, s.shape, 1)
            return jnp.where(key <= query, s, MASKED)

        m_old = m_scr[:, cols]
        m_new = jnp.maximum(m_old, jnp.max(load_scores(), axis=0, keepdims=True))
        alpha = jnp.exp2(m_old - m_new)
        p = jnp.exp2(load_scores() - m_new)
        m_scr[:, cols] = m_new
        acc_scr[:, cols] = alpha * acc_scr[:, cols] + jnp.dot(vt, p.astype(BF16), preferred_element_type=F32)

    n_slots = s_scr.shape[0]
    ahead = n_slots // 2

    def issue_scores(base, item, slot):
        k = keys_of(base, item)[0]
        s_scr[slot, 0:k.shape[0], :] = scores(k, item[1])

    def run(base, items, next_items):
        assert len(items) % n_slots == 0 or not next_items
        upcoming = items + next_items
        for n, item in enumerate(items):
            if n + ahead < len(upcoming):
                issue_scores(base, upcoming[n + ahead], (n + ahead) % n_slots)
            vt = keys_of(base, item)[1]
            chunk_update(item[1], s_scr.at[n % n_slots, 0:vt.shape[1], :], vt, item[2])

    visible_items = [(d * tk, c, None) for d in range(t // tk) for c in range(n_chunks)]
    diagonal_items = []
    for d in range(t // tk):
        for c in range(n_chunks):
            first_query = (c * cq) % t
            if d * tk <= first_query + cq - 1:
                fully_visible = d * tk + tk - 1 <= first_query
                assert d * tk <= first_query
                diagonal_items.append((d * tk, c, None if fully_visible else d * tk))
    diagonal_items += [(None, c, None) for c in range(n_chunks)]

    m_scr[...] = jnp.full_like(m_scr, MASKED)
    acc_scr[...] = jnp.zeros_like(acc_scr)
    assert [item[:2] for item in diagonal_items[:ahead]] == [item[:2] for item in visible_items[:ahead]]
    for n in range(ahead):
        issue_scores(0, visible_items[n], n)

    def body(g, carry):
        run(pl.multiple_of(g * t, t), visible_items, [(t + rel, c, None) for rel, c, _ in visible_items[:ahead]])
        return carry

    lax.fori_loop(0, i, body, 0)
    run(pl.multiple_of(i * t, t), diagonal_items, [])

    lam = (jnp.exp(jnp.sum(lq1_ref[...] * lk1_ref[...], axis=-1, keepdims=True))
           - jnp.exp(jnp.sum(lq2_ref[...] * lk2_ref[...], axis=-1, keepdims=True)) + LAMBDA_INIT)
    o = acc_scr[0:V_DIM, :] / acc_scr[V_DIM:V_DIM + 1, :]
    out = o[:, 0:t] - lam * o[:, t:2 * t]
    normed = out * lax.rsqrt(jnp.mean(out * out, axis=0, keepdims=True) + RMS_EPS)
    o_ref[...] = ((normed * g_ref[...]) * (1.0 - LAMBDA_INIT)).T.astype(BF16)


def _attention(qk, vt, qk_meta, vt_meta, lq1, lk1, lq2, lk2, subln, *, t=2048, tk=512, cq=2 * MXU_WIDTH,
               cast=(), steps_per_cast=2):
    m = qk.shape[0]
    lam_vec = pl.BlockSpec((1, HEAD_DIM), lambda h, i: (0, 0))
    casts_per_head = m // t // steps_per_cast
    cast_in, cast_out, cast_shapes = _cast_plumbing(
        cast, N_HEADS * casts_per_head, lambda h, i: (h * casts_per_head + i // steps_per_cast, 0))
    return pl.pallas_call(
        functools.partial(_attn_kernel, t=t, tk=tk, cq=cq, n_cast=len(cast), steps_per_cast=steps_per_cast),
        grid=(N_HEADS, m // t),
        in_specs=[
            pl.BlockSpec((t, V_DIM), lambda h, i: (i, h)),
            pl.BlockSpec((m, V_DIM), lambda h, i: (0, N_HEADS + h)),
            pl.BlockSpec((VT_ROWS, m), lambda h, i: (h, 0)),
            pl.BlockSpec((N_META, V_DIM), lambda h, i: (0, N_HEADS + h)),
            pl.BlockSpec((VT_ROWS, N_META), lambda h, i: (h, 0)),
            lam_vec, lam_vec, lam_vec, lam_vec,
            pl.BlockSpec((V_DIM, 1), lambda h, i: (0, 0)),
        ] + cast_in,
        out_specs=[pl.BlockSpec((t, V_DIM), lambda h, i: (i, h))] + cast_out,
        out_shape=[jax.ShapeDtypeStruct((m, ATT_WIDTH), BF16)] + cast_shapes,
        scratch_shapes=[pltpu.VMEM((2 * t, V_DIM), BF16), pltpu.VMEM((4, tk, cq), F32),
                        pltpu.VMEM((1, 2 * t), F32), pltpu.VMEM((VT_ROWS, 2 * t), F32)],
        compiler_params=_params("parallel", "arbitrary"),
        name="diff_attention",
    )(qk, qk, vt, qk_meta, vt_meta,
      lq1.reshape(1, HEAD_DIM), lk1.reshape(1, HEAD_DIM), lq2.reshape(1, HEAD_DIM), lk2.reshape(1, HEAD_DIM),
      subln.reshape(V_DIM, 1), *cast)


def _merge_kernel(h_ref, ya_ref, yc_ref, s_ref, wg_ref, wb_ref, wo_ref, o_ref, *, tn):
    h, ya, yc = h_ref[...], ya_ref[...], yc_ref[...]
    for c in range(D_MODEL // tn):
        cols = slice(c * tn, (c + 1) * tn)
        cols_1 = slice(D_MODEL + c * tn, D_MODEL + (c + 1) * tn)
        gate_a = jax.nn.sigmoid(jnp.dot(h, wg_ref[:, cols], preferred_element_type=F32))
        gate_c = jax.nn.sigmoid(jnp.dot(h, wg_ref[:, cols_1], preferred_element_type=F32))
        z_a = jnp.dot(ya, wb_ref[0, :, cols], preferred_element_type=F32)
        z_c = jnp.dot(yc, wb_ref[1, :, cols], preferred_element_type=F32)
        merged = (gate_a * z_a + gate_c * z_c).astype(BF16)
        term = jnp.dot(merged, wo_ref[cols, :], preferred_element_type=F32)
        if c == 0:
            o_ref[...] = s_ref[...] + term
        else:
            o_ref[...] += term


def _merge(h, y_att, y_conv, s, w_gate, w_branch, w_out, *, tm=256, tn=512):
    m = h.shape[0]
    whole = functools.partial(pl.BlockSpec, pipeline_mode=pl.Buffered(1))
    return pl.pallas_call(
        functools.partial(_merge_kernel, tn=tn),
        grid=(m // tm,),
        in_specs=[
            pl.BlockSpec((tm, D_MODEL), lambda i: (i, 0)),
            pl.BlockSpec((tm, ATT_WIDTH), lambda i: (i, 0)),
            pl.BlockSpec((tm, CONV_WIDTH), lambda i: (i, 0)),
            pl.BlockSpec((tm, D_MODEL), lambda i: (i, 0)),
            whole(w_gate.shape, lambda i: (0, 0)),
            whole(w_branch.shape, lambda i: (0, 0, 0)),
            whole(w_out.shape, lambda i: (0, 0)),
        ],
        out_specs=pl.BlockSpec((tm, D_MODEL), lambda i: (i, 0)),
        out_shape=jax.ShapeDtypeStruct((m, D_MODEL), F32),
        compiler_params=_params("parallel"),
        name="gated_merge",
    )(h, y_att, y_conv, s, w_gate, w_branch, w_out)


def kernel(x, meta_tokens, norm_ffn1, ffn1_w_gu, ffn1_w_down, norm_mix, w_in, conv_w, lambda_q1, lambda_k1,
           lambda_q2, lambda_k2, subln, w_branch, w_gate, w_out, norm_ffn2, ffn2_w_gu, ffn2_w_down, norm_final):
    bsz, seq, _ = x.shape
    assert bsz == 1 and meta_tokens.shape == (N_META, D_MODEL)
    xs = x.reshape(seq, D_MODEL)
    s1_meta, w_gate1, w_up1, w_down1 = _ffn(meta_tokens.astype(x.dtype), norm_ffn1[0], ffn1_w_gu[0], ffn1_w_gu[0],
                                            ffn1_w_down[0], tm=N_META, emit_weights=True)
    s1, w_in_bf = _ffn(xs, norm_ffn1[0], w_gate1, w_up1, w_down1, tm=1024, cast=(w_in[0],))

    _, qk_meta, v_meta, _, meta_tail = _mixer_proj(
        s1_meta, norm_mix[0], w_in_bf, conv_w[0], jnp.zeros((SUBLANES, CONV_WIDTH), F32), tm=N_META, transpose_v=False)
    h, qk, vt, y_conv, _ = _mixer_proj(s1, norm_mix[0], w_in_bf, conv_w[0], meta_tail, tm=512, transpose_v=True)

    later_weights = (w_gate[0], w_branch[0].reshape(N_BRANCH * ATT_WIDTH, D_MODEL), w_out[0],
                     ffn2_w_gu[0], ffn2_w_down[0])
    vt_meta = jnp.concatenate(
        [v_meta.reshape(N_META, N_HEADS, V_DIM).transpose(1, 2, 0),
         jnp.zeros((N_HEADS, BF16_ROWS, N_META), BF16).at[:, 0, :].set(1)], axis=1).reshape(-1, N_META)
    y_att, w_gate_bf, w_branch_bf, w_out_bf, w_gu2, w_d2 = _attention(
        qk, vt, qk_meta, vt_meta, lambda_q1[0], lambda_k1[0], lambda_q2[0], lambda_k2[0], subln[0],
        cast=later_weights)
    s2 = _merge(h, y_att, y_conv, s1, w_gate_bf, w_branch_bf.reshape(N_BRANCH, ATT_WIDTH, D_MODEL), w_out_bf)
    out, = _ffn(s2, norm_ffn2[0], w_gu2, w_gu2, w_d2, norm_out=norm_final, tm=1024)
    return out.reshape(bsz, seq, D_MODEL)
```

```python
import functools
import math

import jax
import jax.numpy as jnp
from jax import lax
from jax.experimental import pallas as pl
from jax.experimental.pallas import tpu as pltpu

D_MODEL = 2048
N_META = 16
HEAD_DIM = 64
V_DIM = 2 * HEAD_DIM
ATT_WIDTH = D_MODEL // 2
N_HEADS = ATT_WIDTH // V_DIM
CONV_WIDTH = D_MODEL // 2
CONV_K = 3
N_BRANCH = 2
D_FF = ((8 * D_MODEL // 3 + 255) // 256) * 256
RMS_EPS = 1e-6
LAMBDA_INIT = 0.8 - 0.6 * math.exp(-0.3 * 0)
MASKED = -1e30
Q_SCALE = math.log2(math.e) * HEAD_DIM ** -0.5

SUBLANES = 8
BF16_ROWS = 16
VT_ROWS = V_DIM + BF16_ROWS
MXU_WIDTH = 256
VMEM_LIMIT_BYTES = 56 * 1024 * 1024

BF16 = jnp.bfloat16
F32 = jnp.float32
NT_DIMS = (((1,), (1,)), ((), ()))


def _rms_scale(x):
    return x * lax.rsqrt(jnp.mean(x * x, axis=-1, keepdims=True) + RMS_EPS)


def _params(*semantics):
    return pltpu.CompilerParams(dimension_semantics=semantics, vmem_limit_bytes=VMEM_LIMIT_BYTES)


def _cast_plumbing(weights, n_blocks, index_map):
    specs, shapes = [], []
    for w in weights:
        rows, cols = w.shape
        assert rows % (n_blocks * BF16_ROWS) == 0, (w.shape, n_blocks)
        specs.append(pl.BlockSpec((rows // n_blocks, cols), index_map))
        shapes.append(jax.ShapeDtypeStruct(w.shape, BF16))
    return specs, list(specs), shapes


def _cast_blocks(src_refs, dst_refs):
    for src, dst in zip(src_refs, dst_refs):
        dst[...] = src[...].astype(BF16)


def _ffn_kernel(x_ref, g_ref, wg_ref, wu_ref, wd_ref, *refs, final, n_cast, emit_weights):
    if final:
        gn_ref, refs = refs[0], refs[1:]
    cast_in, acc_ref, cast_out, refs = refs[:n_cast], refs[n_cast], refs[n_cast + 1:2 * n_cast + 1], refs[2 * n_cast + 1:]
    own_out, (h_scr,) = (refs[:3], refs[3:]) if emit_weights else ((), refs)
    j, last = pl.program_id(1), pl.num_programs(1) - 1
    tm = h_scr.shape[0]
    halves = [slice(r, r + MXU_WIDTH) for r in range(0, tm, MXU_WIDTH)] if tm % MXU_WIDTH == 0 else [slice(0, tm)]

    def weights():
        _cast_blocks(cast_in, cast_out)
        ws = [ref[...].astype(BF16) for ref in (wg_ref, wu_ref, wd_ref)]
        for dst, w in zip(own_out, ws):
            dst[...] = w
        return ws

    def gate_up(h, w_gate, w_up):
        return jnp.dot(h, w_gate, preferred_element_type=F32), jnp.dot(h, w_up, preferred_element_type=F32)

    def down(gate, up, w_down):
        act = (gate * jax.nn.sigmoid(gate) * up).astype(BF16)
        return jnp.dot(act, w_down, preferred_element_type=F32)

    @pl.when(j == 0)
    def _():
        w_gate, w_up, w_down = weights()
        gate_ups = []
        for rows in halves:
            h = (_rms_scale(x_ref[rows, :]) * g_ref[...]).astype(BF16)
            h_scr[rows, :] = h
            gate_ups.append(gate_up(h, w_gate, w_up))
        for rows, (gate, up) in zip(halves, gate_ups):
            acc_ref[rows, :] = down(gate, up, w_down)

    @pl.when(jnp.logical_and(j > 0, j < last))
    def _():
        w_gate, w_up, w_down = weights()
        acc_ref[...] += down(*gate_up(h_scr[...], w_gate, w_up), w_down)

    @pl.when(j == last)
    def _():
        w_gate, w_up, w_down = weights()
        gate_ups = [gate_up(h_scr[rows, :], w_gate, w_up) for rows in halves]
        for rows, (gate, up) in zip(halves, gate_ups):
            s = x_ref[rows, :] + 0.5 * (acc_ref[rows, :] + down(gate, up, w_down))
            acc_ref[rows, :] = _rms_scale(s) * gn_ref[...] if final else s


def _ffn(x, g, w_gate, w_up, w_down, *, tm, norm_out=None, tf=512, cast=(), cast_steps=8, emit_weights=False):
    m = x.shape[0]
    n_f = D_FF // tf
    assert n_f >= max(2, cast_steps)
    final = norm_out is not None
    up_block = w_up.shape[1] // tf - n_f
    row = pl.BlockSpec((tm, D_MODEL), lambda i, j: (i, 0))
    vec = pl.BlockSpec((1, D_MODEL), lambda i, j: (0, 0))
    gate_or_up = pl.BlockSpec((D_MODEL, tf), lambda i, j: (0, j))
    down = pl.BlockSpec((tf, D_MODEL), lambda i, j: (j, 0))
    cast_in, cast_out, cast_shapes = _cast_plumbing(
        cast, m // tm * cast_steps, lambda i, j: (i * cast_steps + jnp.minimum(j, cast_steps - 1), 0))
    if emit_weights:
        assert m == tm
        cast_out = cast_out + [gate_or_up, gate_or_up, down]
        cast_shapes = cast_shapes + [jax.ShapeDtypeStruct(s, BF16) for s in
                                     ((D_MODEL, D_FF), (D_MODEL, D_FF), (D_FF, D_MODEL))]
    return pl.pallas_call(
        functools.partial(_ffn_kernel, final=final, n_cast=len(cast), emit_weights=emit_weights),
        grid=(m // tm, n_f),
        in_specs=[row, vec, gate_or_up, pl.BlockSpec((D_MODEL, tf), lambda i, j: (0, up_block + j)), down]
                 + [vec] * final + cast_in,
        out_specs=[row] + cast_out,
        out_shape=[jax.ShapeDtypeStruct((m, D_MODEL), F32)] + cast_shapes,
        scratch_shapes=[pltpu.VMEM((tm, D_MODEL), BF16)],
        compiler_params=_params("parallel", "arbitrary"),
        name="ffn_final" if final else "ffn_mix",
    )(x, g.reshape(1, D_MODEL), w_gate, w_up, w_down, *([norm_out.reshape(1, D_MODEL)] if final else []), *cast)


def _mixer_proj_kernel(s_ref, g_ref, w_ref, cw_ref, cin_ref, h_ref, qk_ref, v_ref, y_ref, cout_ref, carry_scr, *,
                       transpose_v):
    i = pl.program_id(0)
    tm = s_ref.shape[0]

    @pl.when(i == 0)
    def _():
        carry_scr[...] = cin_ref[...]

    h = (_rms_scale(s_ref[...]) * g_ref[...]).astype(BF16)
    h_ref[...] = h

    def proj(first_col, width):
        return jnp.dot(h, w_ref[:, first_col:first_col + width], preferred_element_type=F32)

    qk_ref[:, 0:ATT_WIDTH] = (proj(0, ATT_WIDTH) * Q_SCALE).astype(BF16)
    qk_ref[:, ATT_WIDTH:2 * ATT_WIDTH] = proj(ATT_WIDTH, ATT_WIDTH).astype(BF16)
    v_proj = proj(2 * ATT_WIDTH, ATT_WIDTH)
    if transpose_v:
        v_t = v_proj.T
        ones_then_zeros = (lax.broadcasted_iota(jnp.int32, (BF16_ROWS, tm), 0) == 0).astype(BF16)
        for head in range(N_HEADS):
            v_ref[head * VT_ROWS:head * VT_ROWS + V_DIM, :] = v_t[head * V_DIM:(head + 1) * V_DIM, :].astype(BF16)
            v_ref[head * VT_ROWS + V_DIM:(head + 1) * VT_ROWS, :] = ones_then_zeros
    else:
        v_ref[...] = v_proj.astype(BF16)

    c0 = 3 * ATT_WIDTH
    b_gate = proj(c0, CONV_WIDTH)
    v = proj(c0 + CONV_WIDTH, CONV_WIDTH) * proj(c0 + 2 * CONV_WIDTH, CONV_WIDTH)
    w0, w1, w2 = cw_ref[0:1, :], cw_ref[1:2, :], cw_ref[2:3, :]

    y = w0 * pltpu.roll(v, 2, axis=0) + w1 * pltpu.roll(v, 1, axis=0) + w2 * v
    y_ref[...] = (b_gate * y).astype(BF16)

    prev = carry_scr[...]
    head = v[0:SUBLANES, :]
    rows = lax.broadcasted_iota(jnp.int32, head.shape, 0)
    v1 = jnp.where(rows >= 1, pltpu.roll(head, 1, axis=0), pltpu.roll(prev, 1, axis=0))
    v2 = jnp.where(rows >= 2, pltpu.roll(head, 2, axis=0), pltpu.roll(prev, 2, axis=0))
    y_head = w0 * v2 + w1 * v1 + w2 * head
    y_ref[0:SUBLANES, :] = (b_gate[0:SUBLANES, :] * y_head).astype(BF16)

    tail = v[tm - SUBLANES:tm, :]
    carry_scr[...] = tail
    cout_ref[...] = tail


def _mixer_proj(s, g, w_in, conv_w, carry_in, *, tm, transpose_v):
    m = s.shape[0]
    whole = functools.partial(pl.BlockSpec, index_map=lambda i: (0, 0), pipeline_mode=pl.Buffered(1))
    row = pl.BlockSpec((tm, D_MODEL), lambda i: (i, 0))
    if transpose_v:
        v_spec, v_shape = pl.BlockSpec((N_HEADS * VT_ROWS, tm), lambda i: (0, i)), (N_HEADS * VT_ROWS, m)
    else:
        v_spec, v_shape = pl.BlockSpec((tm, ATT_WIDTH), lambda i: (i, 0)), (m, ATT_WIDTH)
    return pl.pallas_call(
        functools.partial(_mixer_proj_kernel, transpose_v=transpose_v),
        grid=(m // tm,),
        in_specs=[row, whole((1, D_MODEL)), whole(w_in.shape), whole((CONV_K, CONV_WIDTH)),
                  whole((SUBLANES, CONV_WIDTH))],
        out_specs=(row,
                   pl.BlockSpec((tm, 2 * ATT_WIDTH), lambda i: (i, 0)),
                   v_spec,
                   pl.BlockSpec((tm, CONV_WIDTH), lambda i: (i, 0)),
                   pl.BlockSpec((SUBLANES, CONV_WIDTH), lambda i: (i, 0))),
        out_shape=(jax.ShapeDtypeStruct((m, D_MODEL), BF16),
                   jax.ShapeDtypeStruct((m, 2 * ATT_WIDTH), BF16),
                   jax.ShapeDtypeStruct(v_shape, BF16),
                   jax.ShapeDtypeStruct((m, CONV_WIDTH), BF16),
                   jax.ShapeDtypeStruct((m // tm * SUBLANES, CONV_WIDTH), F32)),
        scratch_shapes=[pltpu.VMEM((SUBLANES, CONV_WIDTH), F32)],
        compiler_params=_params("arbitrary"),
        name="mixer_proj",
    )(s, g.reshape(1, D_MODEL), w_in, conv_w, carry_in)


def _attn_kernel(q_ref, k_ref, vt_ref, km_ref, vtm_ref, lq1_ref, lk1_ref, lq2_ref, lk2_ref, g_ref, *refs,
                 t, tk, cq, n_cast, steps_per_cast):
    cast_in, o_ref, cast_out = refs[:n_cast], refs[n_cast], refs[n_cast + 1:2 * n_cast + 1]
    q2_scr, s_scr, m_scr, acc_scr = refs[2 * n_cast + 1:]
    i = pl.program_id(1)
    n_chunks = 2 * t // cq

    @pl.when(i % steps_per_cast == 0)
    def _():
        _cast_blocks(cast_in, cast_out)

    q = q_ref[...]
    lane = lax.broadcasted_iota(jnp.int32, q.shape, 1)
    zero = jnp.zeros_like(q)
    q2_scr[0:t, :] = jnp.where(lane < HEAD_DIM, q, zero)
    q2_scr[t:2 * t, :] = jnp.where(lane >= HEAD_DIM, q, zero)

    def scores(k, c):
        return lax.dot_general(k, q2_scr[c * cq:(c + 1) * cq, :], NT_DIMS, preferred_element_type=F32)

    def keys_of(base, item):
        if item[0] is None:
            return km_ref[...], vtm_ref[...]
        return k_ref[pl.ds(base + item[0], tk), :], vt_ref[:, pl.ds(base + item[0], tk)]

    def chunk_update(c, s_ref, vt, key_offset):
        cols = slice(c * cq, (c + 1) * cq)

        def load_scores():
            s = s_ref[...]
            if key_offset is None:
                return s
            key = key_offset + lax.broadcasted_iota(jnp.int32, s.shape, 0)
            query = (c * cq) % t + lax.broadcasted_iota(jnp.int32, s.shape, 1)
            return jnp.where(key <= query, s, MASKED)

        m_old = m_scr[:, cols]
        m_new = jnp.maximum(m_old, jnp.max(load_scores(), axis=0, keepdims=True))
        alpha = jnp.exp2(m_old - m_new)
        p = jnp.exp2(load_scores() - m_new)
        m_scr[:, cols] = m_new
        acc_scr[:, cols] = alpha * acc_scr[:, cols] + jnp.dot(vt, p.astype(BF16), preferred_element_type=F32)

    n_slots = s_scr.shape[0]
    ahead = n_slots // 2

    def issue_scores(base, item, slot):
        k = keys_of(base, item)[0]
        s_scr[slot, 0:k.shape[0], :] = scores(k, item[1])

    def run(base, items, next_items):
        assert len(items) % n_slots == 0 or not next_items
        upcoming = items + next_items
        for n, item in enumerate(items):
            if n + ahead < len(upcoming):
                issue_scores(base, upcoming[n + ahead], (n + ahead) % n_slots)
            vt = keys_of(base, item)[1]
            chunk_update(item[1], s_scr.at[n % n_slots, 0:vt.shape[1], :], vt, item[2])

    visible_items = [(d * tk, c, None) for d in range(t // tk) for c in range(n_chunks)]
    diagonal_items = []
    for d in range(t // tk):
        for c in range(n_chunks):
            first_query = (c * cq) % t
            if d * tk <= first_query + cq - 1:
                fully_visible = d * tk + tk - 1 <= first_query
                assert d * tk <= first_query
                diagonal_items.append((d * tk, c, None if fully_visible else d * tk))
    diagonal_items += [(None, c, None) for c in range(n_chunks)]

    m_scr[...] = jnp.full_like(m_scr, MASKED)
    acc_scr[...] = jnp.zeros_like(acc_scr)
    assert [item[:2] for item in diagonal_items[:ahead]] == [item[:2] for item in visible_items[:ahead]]
    for n in range(ahead):
        issue_scores(0, visible_items[n], n)

    def body(g, carry):
        run(pl.multiple_of(g * t, t), visible_items, [(t + rel, c, None) for rel, c, _ in visible_items[:ahead]])
        return carry

    lax.fori_loop(0, i, body, 0)
    run(pl.multiple_of(i * t, t), diagonal_items, [])

    lam = (jnp.exp(jnp.sum(lq1_ref[...] * lk1_ref[...], axis=-1, keepdims=True))
           - jnp.exp(jnp.sum(lq2_ref[...] * lk2_ref[...], axis=-1, keepdims=True)) + LAMBDA_INIT)
    o = acc_scr[0:V_DIM, :] / acc_scr[V_DIM:V_DIM + 1, :]
    out = o[:, 0:t] - lam * o[:, t:2 * t]
    normed = out * lax.rsqrt(jnp.mean(out * out, axis=0, keepdims=True) + RMS_EPS)
    o_ref[...] = ((normed * g_ref[...]) * (1.0 - LAMBDA_INIT)).T.astype(BF16)


def _attention(qk, vt, qk_meta, vt_meta, lq1, lk1, lq2, lk2, subln, *, t=2048, tk=512, cq=2 * MXU_WIDTH,
               cast=(), steps_per_cast=2):
    m = qk.shape[0]
    lam_vec = pl.BlockSpec((1, HEAD_DIM), lambda h, i: (0, 0))
    casts_per_head = m // t // steps_per_cast
    cast_in, cast_out, cast_shapes = _cast_plumbing(
        cast, N_HEADS * casts_per_head, lambda h, i: (h * casts_per_head + i // steps_per_cast, 0))
    return pl.pallas_call(
        functools.partial(_attn_kernel, t=t, tk=tk, cq=cq, n_cast=len(cast), steps_per_cast=steps_per_cast),
        grid=(N_HEADS, m // t),
        in_specs=[
            pl.BlockSpec((t, V_DIM), lambda h, i: (i, h)),
            pl.BlockSpec((m, V_DIM), lambda h, i: (0, N_HEADS + h)),
            pl.BlockSpec((VT_ROWS, m), lambda h, i: (h, 0)),
            pl.BlockSpec((N_META, V_DIM), lambda h, i: (0, N_HEADS + h)),
            pl.BlockSpec((VT_ROWS, N_META), lambda h, i: (h, 0)),
            lam_vec, lam_vec, lam_vec, lam_vec,
            pl.BlockSpec((V_DIM, 1), lambda h, i: (0, 0)),
        ] + cast_in,
        out_specs=[pl.BlockSpec((t, V_DIM), lambda h, i: (i, h))] + cast_out,
        out_shape=[jax.ShapeDtypeStruct((m, ATT_WIDTH), BF16)] + cast_shapes,
        scratch_shapes=[pltpu.VMEM((2 * t, V_DIM), BF16), pltpu.VMEM((4, tk, cq), F32),
                        pltpu.VMEM((1, 2 * t), F32), pltpu.VMEM((VT_ROWS, 2 * t), F32)],
        compiler_params=_params("parallel", "arbitrary"),
        name="diff_attention",
    )(qk, qk, vt, qk_meta, vt_meta,
      lq1.reshape(1, HEAD_DIM), lk1.reshape(1, HEAD_DIM), lq2.reshape(1, HEAD_DIM), lk2.reshape(1, HEAD_DIM),
      subln.reshape(V_DIM, 1), *cast)


def _merge_kernel(h_ref, ya_ref, yc_ref, s_ref, wg_ref, wb_ref, wo_ref, o_ref, *, tn):
    h, ya, yc = h_ref[...], ya_ref[...], yc_ref[...]
    for c in range(D_MODEL // tn):
        cols = slice(c * tn, (c + 1) * tn)
        cols_1 = slice(D_MODEL + c * tn, D_MODEL + (c + 1) * tn)
        gate_a = jax.nn.sigmoid(jnp.dot(h, wg_ref[:, cols], preferred_element_type=F32))
        gate_c = jax.nn.sigmoid(jnp.dot(h, wg_ref[:, cols_1], preferred_element_type=F32))
        z_a = jnp.dot(ya, wb_ref[0, :, cols], preferred_element_type=F32)
        z_c = jnp.dot(yc, wb_ref[1, :, cols], preferred_element_type=F32)
        merged = (gate_a * z_a + gate_c * z_c).astype(BF16)
        term = jnp.dot(merged, wo_ref[cols, :], preferred_element_type=F32)
        if c == 0:
            o_ref[...] = s_ref[...] + term
        else:
            o_ref[...] += term


def _merge(h, y_att, y_conv, s, w_gate, w_branch, w_out, *, tm=256, tn=512):
    m = h.shape[0]
    whole = functools.partial(pl.BlockSpec, pipeline_mode=pl.Buffered(1))
    return pl.pallas_call(
        functools.partial(_merge_kernel, tn=tn),
        grid=(m // tm,),
        in_specs=[
            pl.BlockSpec((tm, D_MODEL), lambda i: (i, 0)),
            pl.BlockSpec((tm, ATT_WIDTH), lambda i: (i, 0)),
            pl.BlockSpec((tm, CONV_WIDTH), lambda i: (i, 0)),
            pl.BlockSpec((tm, D_MODEL), lambda i: (i, 0)),
            whole(w_gate.shape, lambda i: (0, 0)),
            whole(w_branch.shape, lambda i: (0, 0, 0)),
            whole(w_out.shape, lambda i: (0, 0)),
        ],
        out_specs=pl.BlockSpec((tm, D_MODEL), lambda i: (i, 0)),
        out_shape=jax.ShapeDtypeStruct((m, D_MODEL), F32),
        compiler_params=_params("parallel"),
        name="gated_merge",
    )(h, y_att, y_conv, s, w_gate, w_branch, w_out)


def kernel(x, meta_tokens, norm_ffn1, ffn1_w_gu, ffn1_w_down, norm_mix, w_in, conv_w, lambda_q1, lambda_k1,
           lambda_q2, lambda_k2, subln, w_branch, w_gate, w_out, norm_ffn2, ffn2_w_gu, ffn2_w_down, norm_final):
    bsz, seq, _ = x.shape
    assert bsz == 1 and meta_tokens.shape == (N_META, D_MODEL)
    xs = x.reshape(seq, D_MODEL)
    s1_meta, w_gate1, w_up1, w_down1 = _ffn(meta_tokens.astype(x.dtype), norm_ffn1[0], ffn1_w_gu[0], ffn1_w_gu[0],
                                            ffn1_w_down[0], tm=N_META, emit_weights=True)
    s1, w_in_bf = _ffn(xs, norm_ffn1[0], w_gate1, w_up1, w_down1, tm=1024, cast=(w_in[0],))

    _, qk_meta, v_meta, _, meta_tail = _mixer_proj(
        s1_meta, norm_mix[0], w_in_bf, conv_w[0], jnp.zeros((SUBLANES, CONV_WIDTH), F32), tm=N_META, transpose_v=False)
    h, qk, vt, y_conv, _ = _mixer_proj(s1, norm_mix[0], w_in_bf, conv_w[0], meta_tail, tm=512, transpose_v=True)

    later_weights = (w_gate[0], w_branch[0].reshape(N_BRANCH * ATT_WIDTH, D_MODEL), w_out[0],
                     ffn2_w_gu[0], ffn2_w_down[0])
    vt_meta = jnp.concatenate(
        [v_meta.reshape(N_META, N_HEADS, V_DIM).transpose(1, 2, 0),
         jnp.zeros((N_HEADS, BF16_ROWS, N_META), BF16).at[:, 0, :].set(1)], axis=1).reshape(-1, N_META)
    y_att, w_gate_bf, w_branch_bf, w_out_bf, w_gu2, w_d2 = _attention(
        qk, vt, qk_meta, vt_meta, lambda_q1[0], lambda_k1[0], lambda_q2[0], lambda_k2[0], subln[0],
        cast=later_weights)
    s2 = _merge(h, y_att, y_conv, s1, w_gate_bf, w_branch_bf.reshape(N_BRANCH, ATT_WIDTH, D_MODEL), w_out_bf)
    out, = _ffn(s2, norm_ffn2[0], w_gu2, w_gu2, w_d2, norm_out=norm_final, tm=1024)
    return out.reshape(bsz, seq, D_MODEL)
```
